```python
import math
import jax
import jax.numpy as jnp
from jax import lax
import numpy as np

D_MODEL = 4096
BATCH = 4
SEQ = 2048
DEPTH = 4
DEC_BATCH = 128
DEC_SEQ = 8
PAST_LEN = 16384
PAGE_SIZE = 128

N_MIXERS = 2
N_A_LAYERS = (DEPTH + 1) // 2
N_B_LAYERS = DEPTH // 2
CONV_W = 4
CHUNK = 64
EPS = 1e-6

DN_DK = 128
DN_DV = 128
DN_HK = D_MODEL // 128
DN_HV = 2 * DN_HK
DN_KEY = DN_HK * DN_DK
DN_VAL = DN_HV * DN_DV
DN_CONV_DIM = 2 * DN_KEY + DN_VAL
DN_IN = DN_CONV_DIM + DN_VAL + 2 * DN_HV

SSM_INNER = 2 * D_MODEL
SSM_P = 64
SSM_HEADS = SSM_INNER // SSM_P
SSM_GROUPS = 8
SSM_N = 128
SSM_CONV_DIM = SSM_INNER + 2 * SSM_GROUPS * SSM_N
SSM_IN = SSM_INNER + SSM_CONV_DIM + SSM_HEADS

kernel_name = 'hybrid_gdn_ssd_adaln_step'


def rmsnorm(x, w):
    xf = x.astype(jnp.float32)
    y = xf * lax.rsqrt(jnp.mean(xf * xf, axis=-1, keepdims=True) + EPS)
    return (y * w.astype(jnp.float32)).astype(x.dtype)


def l2norm(x):
    xf = x.astype(jnp.float32)
    return xf * lax.rsqrt(jnp.sum(xf * xf, axis=-1, keepdims=True) + EPS)


def causal_conv(x, buf, w):
    xp = jnp.concatenate([buf.astype(x.dtype), x], axis=1)
    y = lax.conv_general_dilated(xp, w[:, None, :].astype(x.dtype), window_strides=(1,), padding='VALID',
                                 dimension_numbers=('NWC', 'WIO', 'NWC'), feature_group_count=x.shape[-1])
    return y, xp[:, xp.shape[1] - (CONV_W - 1):]


def chunk_layout(L):
    c = min(CHUNK, L)
    n = -(-L // c)
    return c, n, n * c - L


def to_chunks(t, c, n, pad):
    t = jnp.pad(t.astype(jnp.float32), [(0, 0), (0, pad)] + [(0, 0)] * (t.ndim - 2))
    return jnp.moveaxis(t.reshape((t.shape[0], n, c) + t.shape[2:]), 1, 0)


def gated_delta_rule(q, k, v, beta, g, S0):
    b, L = q.shape[:2]
    c, n, pad = chunk_layout(L)
    qs, ks, vs, bs, gs = [jnp.swapaxes(to_chunks(t, c, n, pad), 2, 3) for t in (q, k, v, beta, g)]
    causal = jnp.tril(jnp.ones((c, c), dtype=bool))
    strict = jnp.tril(jnp.ones((c, c), dtype=bool), -1)
    eye = jnp.eye(c, dtype=jnp.float32)

    def step(S, xs):
        qc, kc, vc, bc, gc = xs
        gcum = jnp.cumsum(gc, axis=-1)
        decay = jnp.exp(jnp.where(causal, gcum[..., :, None] - gcum[..., None, :], -jnp.inf))
        lower = jnp.where(strict, bc[..., :, None] * jnp.einsum('bhid,bhjd->bhij', kc, kc) * decay, 0.0)
        rhs = jnp.concatenate([bc[..., None] * vc, (bc * jnp.exp(gcum))[..., None] * kc], axis=-1)
        sol = lax.linalg.triangular_solve(eye + lower, rhs, left_side=True, lower=True)
        u = sol[..., :DN_DV] - jnp.einsum('bhcd,bhde->bhce', sol[..., DN_DV:], S)
        scores = jnp.einsum('bhid,bhjd->bhij', qc, kc) * decay
        o = (jnp.einsum('bhcd,bhde->bhce', qc * jnp.exp(gcum)[..., None], S)
             + jnp.einsum('bhij,bhje->bhie', scores, u))
        g_last = gcum[..., -1:]
        S_new = (S * jnp.exp(g_last)[..., None]
                 + jnp.einsum('bhcd,bhce->bhde', kc * jnp.exp(g_last - gcum)[..., None], u))
        return S_new, o

    S_fin, os = lax.scan(step, S0.astype(jnp.float32), (qs, ks, vs, bs, gs))
    o = jnp.transpose(os, (1, 0, 3, 2, 4)).reshape(b, n * c, DN_HV, DN_DV)[:, :L]
    return o, S_fin


def ssd_scan(x, dt, A, Bm, Cm, h0):
    b, L = x.shape[:2]
    R = SSM_HEADS // SSM_GROUPS
    c, n, pad = chunk_layout(L)
    xs = to_chunks(x, c, n, pad).reshape(n, b, c, SSM_GROUPS, R, SSM_P)
    dts = to_chunks(dt, c, n, pad).reshape(n, b, c, SSM_GROUPS, R)
    Bs = to_chunks(Bm, c, n, pad)
    Cs = to_chunks(Cm, c, n, pad)
    a_gr = A.astype(jnp.float32).reshape(SSM_GROUPS, R)
    causal = jnp.tril(jnp.ones((c, c), dtype=bool))[None, :, :, None, None]

    def step(hs, inp):
        xc, dtc, bc, cc = inp
        cs = jnp.cumsum(dtc * a_gr, axis=1)
        seg = jnp.exp(jnp.where(causal, cs[:, :, None] - cs[:, None, :], -jnp.inf))
        cb = jnp.einsum('bign,bjgn->bijg', cc, bc)
        y = jnp.einsum('bijg,bijgr,bjgr,bjgrp->bigrp', cb, seg, dtc, xc)
        y = y + jnp.einsum('bign,bgrpn,bigr->bigrp', cc, hs, jnp.exp(cs))
        cs_last = cs[:, -1]
        hs_new = (hs * jnp.exp(cs_last)[..., None, None]
                  + jnp.einsum('bjgn,bjgr,bjgrp->bgrpn', bc, jnp.exp(cs_last[:, None] - cs) * dtc, xc))
        return hs_new, y

    h_fin, ys = lax.scan(step, h0.astype(jnp.float32).reshape(b, SSM_GROUPS, R, SSM_P, SSM_N), (xs, dts, Bs, Cs))
    y = jnp.moveaxis(ys, 0, 1).reshape(b, n * c, SSM_HEADS, SSM_P)[:, :L]
    return y, h_fin.reshape(b, SSM_HEADS, SSM_P, SSM_N)


def deltanet_mixer(h, S0, conv0, w_in, w_conv, a_log, dt_bias, norm_w, w_out):
    b, L, _ = h.shape
    proj = h @ w_in
    qkv, z, bb, aa = jnp.split(proj, [DN_CONV_DIM, DN_CONV_DIM + DN_VAL, DN_CONV_DIM + DN_VAL + DN_HV], axis=-1)
    qkv, conv_new = causal_conv(qkv, conv0, w_conv)
    qkv = jax.nn.silu(qkv)
    q, k, v = jnp.split(qkv, [DN_KEY, 2 * DN_KEY], axis=-1)
    rep = DN_HV // DN_HK
    q = jnp.repeat(l2norm(q.reshape(b, L, DN_HK, DN_DK)) * (DN_DK ** -0.5), rep, axis=2)
    k = jnp.repeat(l2norm(k.reshape(b, L, DN_HK, DN_DK)), rep, axis=2)
    v = v.reshape(b, L, DN_HV, DN_DV)
    beta = jax.nn.sigmoid(bb.astype(jnp.float32))
    g = -jnp.exp(a_log.astype(jnp.float32)) * jax.nn.softplus(aa.astype(jnp.float32) + dt_bias.astype(jnp.float32))
    o, S_new = gated_delta_rule(q, k, v, beta, g, S0)
    o = rmsnorm(o, norm_w) * jax.nn.silu(z.astype(jnp.float32).reshape(b, L, DN_HV, DN_DV))
    return o.reshape(b, L, DN_VAL).astype(h.dtype) @ w_out, S_new, conv_new


def mamba2_mixer(h, h0, conv0, w_in, w_conv, conv_b, a_log, dt_bias, d_skip, norm_w, w_out):
    b, L, _ = h.shape
    proj = h @ w_in
    z, xbc, dt = jnp.split(proj, [SSM_INNER, SSM_INNER + SSM_CONV_DIM], axis=-1)
    xbc, conv_new = causal_conv(xbc, conv0, w_conv)
    xbc = jax.nn.silu(xbc + conv_b)
    xs, Bm, Cm = jnp.split(xbc, [SSM_INNER, SSM_INNER + SSM_GROUPS * SSM_N], axis=-1)
    xs = xs.reshape(b, L, SSM_HEADS, SSM_P)
    Bm = Bm.reshape(b, L, SSM_GROUPS, SSM_N)
    Cm = Cm.reshape(b, L, SSM_GROUPS, SSM_N)
    dt = jax.nn.softplus(dt.astype(jnp.float32) + dt_bias.astype(jnp.float32))
    A = -jnp.exp(a_log.astype(jnp.float32))
    y, h_new = ssd_scan(xs, dt, A, Bm, Cm, h0)
    y = y + d_skip.astype(jnp.float32)[:, None] * xs.astype(jnp.float32)
    y = y.reshape(b, L, SSM_INNER) * jax.nn.silu(z.astype(jnp.float32))
    y = rmsnorm(y.reshape(b, L, SSM_GROUPS, SSM_INNER // SSM_GROUPS),
                norm_w.reshape(SSM_GROUPS, SSM_INNER // SSM_GROUPS)).reshape(b, L, SSM_INNER)
    return y.astype(h.dtype) @ w_out, h_new, conv_new


def trunk(x, c, dn_S, dn_conv, ssm_h, ssm_conv, w_ada, b_ada, norm_w,
          dn_w_in, dn_w_conv, dn_a_log, dn_dt_bias, dn_norm_w, dn_w_out,
          ssm_w_in, ssm_w_conv, ssm_conv_b, ssm_a_log, ssm_dt_bias, ssm_d, ssm_norm_w, ssm_w_out,
          final_norm_w):
    c_act = jax.nn.silu(c)
    new_dn_S, new_dn_conv, new_ssm, new_ssm_conv = [], [], [], []
    for i in range(DEPTH):
        j = i // N_MIXERS
        mod = (c_act @ w_ada[i] + b_ada[i])[:, None, :]
        shift, scale, gate = jnp.split(mod, 3, axis=-1)
        h = rmsnorm(x, norm_w[i]) * (1 + scale) + shift
        if i % N_MIXERS == 0:
            out, s_new, cv_new = deltanet_mixer(h, dn_S[j], dn_conv[j], dn_w_in[j], dn_w_conv[j], dn_a_log[j],
                                                dn_dt_bias[j], dn_norm_w[j], dn_w_out[j])
            new_dn_S.append(s_new.astype(dn_S.dtype))
            new_dn_conv.append(cv_new.astype(dn_conv.dtype))
        else:
            out, s_new, cv_new = mamba2_mixer(h, ssm_h[j], ssm_conv[j], ssm_w_in[j], ssm_w_conv[j], ssm_conv_b[j],
                                              ssm_a_log[j], ssm_dt_bias[j], ssm_d[j], ssm_norm_w[j], ssm_w_out[j])
            new_ssm.append(s_new.astype(ssm_h.dtype))
            new_ssm_conv.append(cv_new.astype(ssm_conv.dtype))
        x = x + gate * out
    y = rmsnorm(x, final_norm_w)
    return y, jnp.stack(new_dn_S), jnp.stack(new_dn_conv), jnp.stack(new_ssm), jnp.stack(new_ssm_conv)


def setup_inputs(seed: int = 0) -> dict:
    key = jax.random.key(seed)
    ks = jax.random.split(key, 32)
    f32 = jnp.float32

    def nrm(k, shape, scale):
        return jax.random.normal(k, shape, f32) * scale

    def dt_bias_init(k, shape):
        dt = jnp.exp(jax.random.uniform(k, shape, f32, math.log(1e-3), math.log(1e-1)))
        return dt + jnp.log(-jnp.expm1(-dt))

    def a_log_init(k, shape):
        return jnp.log(jax.random.uniform(k, shape, f32, 1.0, 16.0))

    return {
        'x_prompt': nrm(ks[0], (BATCH, SEQ, D_MODEL), 1.0),
        'x_sample': nrm(ks[1], (DEC_BATCH, DEC_SEQ, D_MODEL), 1.0),
        'state_dn_S': nrm(ks[2], (N_A_LAYERS, DEC_BATCH, DN_HV, DN_DK, DN_DV), 0.3),
        'state_dn_conv': nrm(ks[3], (N_A_LAYERS, DEC_BATCH, CONV_W - 1, DN_CONV_DIM), 1.0),
        'state_ssm': nrm(ks[4], (N_B_LAYERS, DEC_BATCH, SSM_HEADS, SSM_P, SSM_N), 0.1),
        'state_ssm_conv': nrm(ks[5], (N_B_LAYERS, DEC_BATCH, CONV_W - 1, SSM_CONV_DIM), 1.0),
        'c_prompt': nrm(ks[6], (BATCH, D_MODEL), 1.0),
        'c_sample': nrm(ks[7], (DEC_BATCH, D_MODEL), 1.0),
        'w_ada': nrm(ks[8], (DEPTH, D_MODEL, 3 * D_MODEL), 0.5 * D_MODEL ** -0.5),
        'b_ada': nrm(ks[9], (DEPTH, 3 * D_MODEL), 0.02),
        'norm_w': 1.0 + nrm(ks[10], (DEPTH, D_MODEL), 0.02),
        'dn_w_in': nrm(ks[11], (N_A_LAYERS, D_MODEL, DN_IN), D_MODEL ** -0.5),
        'dn_w_conv': nrm(ks[12], (N_A_LAYERS, CONV_W, DN_CONV_DIM), CONV_W ** -0.5),
        'dn_a_log': a_log_init(ks[13], (N_A_LAYERS, DN_HV)),
        'dn_dt_bias': dt_bias_init(ks[14], (N_A_LAYERS, DN_HV)),
        'dn_norm_w': 1.0 + nrm(ks[15], (N_A_LAYERS, DN_DV), 0.02),
        'dn_w_out': nrm(ks[16], (N_A_LAYERS, DN_VAL, D_MODEL), DN_VAL ** -0.5),
        'ssm_w_in': nrm(ks[17], (N_B_LAYERS, D_MODEL, SSM_IN), D_MODEL ** -0.5),
        'ssm_w_conv': nrm(ks[18], (N_B_LAYERS, CONV_W, SSM_CONV_DIM), CONV_W ** -0.5),
        'ssm_conv_b': nrm(ks[19], (N_B_LAYERS, SSM_CONV_DIM), 0.02),
        'ssm_a_log': a_log_init(ks[20], (N_B_LAYERS, SSM_HEADS)),
        'ssm_dt_bias': dt_bias_init(ks[21], (N_B_LAYERS, SSM_HEADS)),
        'ssm_d': 1.0 + nrm(ks[22], (N_B_LAYERS, SSM_HEADS), 0.1),
        'ssm_norm_w': 1.0 + nrm(ks[23], (N_B_LAYERS, SSM_INNER), 0.02),
        'ssm_w_out': nrm(ks[24], (N_B_LAYERS, SSM_INNER, D_MODEL), SSM_INNER ** -0.5),
        'final_norm_w': 1.0 + nrm(ks[25], (D_MODEL,), 0.02),
    }


def reference(x_prompt, x_sample, state_dn_S, state_dn_conv, state_ssm, state_ssm_conv, c_prompt, c_sample,
              w_ada, b_ada, norm_w, dn_w_in, dn_w_conv, dn_a_log, dn_dt_bias, dn_norm_w, dn_w_out,
              ssm_w_in, ssm_w_conv, ssm_conv_b, ssm_a_log, ssm_dt_bias, ssm_d, ssm_norm_w, ssm_w_out,
              final_norm_w):
    weights = (w_ada, b_ada, norm_w, dn_w_in, dn_w_conv, dn_a_log, dn_dt_bias, dn_norm_w, dn_w_out,
               ssm_w_in, ssm_w_conv, ssm_conv_b, ssm_a_log, ssm_dt_bias, ssm_d, ssm_norm_w, ssm_w_out,
               final_norm_w)
    bp = x_prompt.shape[0]
    zero_dn_S = jnp.zeros((N_A_LAYERS, bp, DN_HV, DN_DK, DN_DV), state_dn_S.dtype)
    zero_dn_conv = jnp.zeros((N_A_LAYERS, bp, CONV_W - 1, DN_CONV_DIM), x_prompt.dtype)
    zero_ssm = jnp.zeros((N_B_LAYERS, bp, SSM_HEADS, SSM_P, SSM_N), state_ssm.dtype)
    zero_ssm_conv = jnp.zeros((N_B_LAYERS, bp, CONV_W - 1, SSM_CONV_DIM), x_prompt.dtype)
    y_prompt, dn_S_p, dn_conv_p, ssm_p, ssm_conv_p = trunk(
        x_prompt, c_prompt, zero_dn_S, zero_dn_conv, zero_ssm, zero_ssm_conv, *weights)
    y_sample, dn_S_s, dn_conv_s, ssm_s, ssm_conv_s = trunk(
        x_sample, c_sample, state_dn_S, state_dn_conv, state_ssm, state_ssm_conv, *weights)
    return (y_prompt, y_sample, dn_S_p, dn_conv_p, ssm_p, ssm_conv_p, dn_S_s, dn_conv_s, ssm_s, ssm_conv_s)
```

```python
import functools

import jax
import jax.numpy as jnp
from jax import lax
from jax.experimental import pallas as pl
from jax.experimental.pallas import tpu as pltpu

F32 = jnp.float32
BF16 = jnp.bfloat16

EPS = 1e-6
CONV_W = 4
CHUNK = 64
DN_DK = 128
DN_DV = 128
SSM_P = 64
SSM_N = 128
SSM_GROUPS = 8
ROWS = 8
MIB = 1024 * 1024


def _cparams(n_axes, vmem_mib):
    return pltpu.CompilerParams(dimension_semantics=("arbitrary",) * n_axes,
                                vmem_limit_bytes=vmem_mib * MIB)


def _silu(x):
    return x * jax.nn.sigmoid(x)


def _softplus(x):
    return jnp.maximum(x, 0.0) + jnp.log1p(jnp.exp(-jnp.abs(x)))


def _mm(a, b):
    return jnp.dot(a, b, preferred_element_type=F32)


def _dot(a, b):
    return _mm(a.astype(BF16), b.astype(BF16))


def _dot_nt(a, b):
    return lax.dot_general(a.astype(BF16), b.astype(BF16), (((1,), (1,)), ((), ())),
                           preferred_element_type=F32)


def _dot_tn(a, b):
    return lax.dot_general(a.astype(BF16), b.astype(BF16), (((0,), (0,)), ((), ())),
                           preferred_element_type=F32)


def _split2(a):
    hi = a.astype(BF16)
    lo = (a - hi.astype(F32)).astype(BF16)
    return hi, lo


def _split3(a):
    hi = a.astype(BF16)
    r = a - hi.astype(F32)
    mid = r.astype(BF16)
    lo = (r - mid.astype(F32)).astype(BF16)
    return hi, mid, lo


def _dot3(a, b):
    ah, al = _split2(a)
    bh, bl = _split2(b)
    return _mm(ah, bh) + (_mm(ah, bl) + _mm(al, bh))


def _cumsum_rows(tri, g):
    hi, mid, lo = _split3(g)
    return _mm(tri, hi) + (_mm(tri, mid) + _mm(tri, lo))


def _cumsum_cols(g, tri_t):
    hi, mid, lo = _split3(g)
    return _mm(hi, tri_t) + (_mm(mid, tri_t) + _mm(lo, tri_t))


def _unit_lower_solve(lm, w, c, row, col):
    bs = min(16, c)
    eye = jnp.where(row == col, 1.0, 0.0).astype(F32)
    same = (row // bs) == (col // bs)
    d = jnp.where(same, lm, 0.0)
    p = eye - d
    q = d
    n = 2
    while n < bs:
        q = _dot3(q, q)
        p = p + _dot3(p, q)
        n *= 2
    t = _dot3(p, w)
    nb = c // bs
    if nb == 1:
        return t
    m = _dot3(p, lm - d)
    t = t - _dot3(m, t)
    n = 2
    while n < nb:
        m = _dot3(m, m)
        t = t + _dot3(m, t)
        n *= 2
    return t


def _mod_kernel(c_ref, w_ref, b_ref, o_ref):
    ca = _silu(c_ref[...]).astype(BF16)
    o_ref[...] = _mm(ca, w_ref[...].astype(BF16)) + b_ref[...]


def _mod_call(c_all, w_ada, b_ada):
    depth, d, n3 = w_ada.shape
    rows = c_all.shape[0]
    tn = 512
    return pl.pallas_call(
        _mod_kernel,
        out_shape=jax.ShapeDtypeStruct((depth, rows, n3), F32),
        grid=(depth, n3 // tn),
        in_specs=[pl.BlockSpec((rows, d), lambda i, n: (0, 0)),
                  pl.BlockSpec((None, d, tn), lambda i, n: (i, 0, n)),
                  pl.BlockSpec((None, 1, tn), lambda i, n: (i, 0, n))],
        out_specs=pl.BlockSpec((None, rows, tn), lambda i, n: (i, 0, n)),
        compiler_params=_cparams(2, 40),
        name="adaln_mod",
    )(c_all, w_ada, b_ada.reshape(depth, 1, n3))


def _prenorm_kernel(x_ref, nw_ref, sc_ref, sh_ref, o_ref):
    x = x_ref[...]
    y = x * lax.rsqrt(jnp.mean(x * x, axis=-1, keepdims=True) + EPS) * nw_ref[...]
    h = y * (1.0 + sc_ref[...]) + sh_ref[...]
    o_ref[...] = h.reshape(o_ref.shape).astype(BF16)


def _prenorm_call(x3, nw, scale_e, shift_e):
    mb, rows, d = x3.shape
    tb = 32
    return pl.pallas_call(
        _prenorm_kernel,
        out_shape=jax.ShapeDtypeStruct((mb * rows, d), BF16),
        grid=(mb // tb,),
        in_specs=[pl.BlockSpec((tb, rows, d), lambda t: (t, 0, 0)),
                  pl.BlockSpec((1, 1, d), lambda t: (0, 0, 0)),
                  pl.BlockSpec((tb, 1, d), lambda t: (t, 0, 0)),
                  pl.BlockSpec((tb, 1, d), lambda t: (t, 0, 0))],
        out_specs=pl.BlockSpec((tb * rows, d), lambda t: (t, 0)),
        compiler_params=_cparams(1, 40),
        name="prenorm",
    )(x3, nw.reshape(1, 1, d), scale_e, shift_e)


def _final_norm_kernel(x_ref, nw_ref, o_ref):
    x = x_ref[...]
    o_ref[...] = x * lax.rsqrt(jnp.mean(x * x, axis=-1, keepdims=True) + EPS) * nw_ref[...]


def _final_norm_call(x2, nw, row0, rows):
    d = x2.shape[1]
    tm = 512
    return pl.pallas_call(
        _final_norm_kernel,
        out_shape=jax.ShapeDtypeStruct((rows, d), F32),
        grid=(rows // tm,),
        in_specs=[pl.BlockSpec((tm, d), lambda t: (t + row0 // tm, 0)),
                  pl.BlockSpec((1, d), lambda t: (0, 0))],
        out_specs=pl.BlockSpec((tm, d), lambda t: (t, 0)),
        compiler_params=_cparams(1, 40),
        name="final_norm",
    )(x2, nw.reshape(1, d))


def _mm_in_kernel(h_ref, w_ref, o_ref, wb_ref):
    @pl.when(pl.program_id(1) == 0)
    def _():
        wb_ref[...] = w_ref[...].astype(BF16)

    o_ref[...] = _mm(h_ref[...], wb_ref[...])


def _mm_in_call(h, w, j, n_cols):
    m, k = h.shape
    tm, tn = 1024, 512
    return pl.pallas_call(
        _mm_in_kernel,
        out_shape=jax.ShapeDtypeStruct((m, n_cols), F32),
        grid=(n_cols // tn, m // tm),
        in_specs=[pl.BlockSpec((tm, k), lambda n, i: (i, 0)),
                  pl.BlockSpec((None, k, tn), lambda n, i: (j, 0, n))],
        out_specs=pl.BlockSpec((tm, tn), lambda n, i: (i, n)),
        scratch_shapes=[pltpu.VMEM((k, tn), BF16)],
        compiler_params=_cparams(2, 48),
        name="in_proj",
    )(h, w)


def _tail_kernel(h_ref, w_ref, wt_ref, o_ref, ot_ref):
    h = h_ref[...]
    o_ref[...] = _mm(h, w_ref[...].astype(BF16))
    ot_ref[...] = lax.dot_general(wt_ref[...].astype(BF16), h, (((1,), (1,)), ((), ())),
                                  preferred_element_type=F32)


def _tail_call(h, w, j, col0):
    m, k = h.shape
    tm = 1024
    wt = jnp.transpose(w[j, :, col0:col0 + 128])
    return pl.pallas_call(
        _tail_kernel,
        out_shape=(jax.ShapeDtypeStruct((m, 128), F32), jax.ShapeDtypeStruct((128, m), F32)),
        grid=(m // tm,),
        in_specs=[pl.BlockSpec((tm, k), lambda i: (i, 0)),
                  pl.BlockSpec((None, k, 128), lambda i: (j, 0, col0 // 128)),
                  pl.BlockSpec((128, k), lambda i: (0, 0))],
        out_specs=(pl.BlockSpec((tm, 128), lambda i: (i, 0)),
                   pl.BlockSpec((128, tm), lambda i: (0, i))),
        compiler_params=_cparams(1, 40),
        name="in_proj_tail",
    )(h, w, wt)


def _mm_out_kernel(ap_ref, as_ref, w_ref, x_ref, g_ref, o_ref, *, prompt_tiles):
    i = pl.program_id(1)

    def finish(acc):
        tm, tn = acc.shape
        o_ref[...] = x_ref[...] + g_ref[...] * acc.reshape(tm // ROWS, ROWS, tn)

    @pl.when(i < prompt_tiles)
    def _():
        finish(_mm(ap_ref[...], w_ref[...]))

    @pl.when(i >= prompt_tiles)
    def _():
        finish(_mm(as_ref[...], w_ref[...]))


def _mm_out_call(a_p, a_s, w_bf16, x3, gate_e):
    mp, k = a_p.shape
    ms = a_s.shape[0]
    d = w_bf16.shape[1]
    tm, tn = 256, 512
    pt, st = mp // tm, ms // tm
    return pl.pallas_call(
        functools.partial(_mm_out_kernel, prompt_tiles=pt),
        out_shape=jax.ShapeDtypeStruct(x3.shape, F32),
        grid=(d // tn, pt + st),
        in_specs=[pl.BlockSpec((tm, k), lambda n, i: (jnp.minimum(i, pt - 1), 0)),
                  pl.BlockSpec((tm, k), lambda n, i: (jnp.maximum(i - pt, 0), 0)),
                  pl.BlockSpec((k, tn), lambda n, i: (0, n)),
                  pl.BlockSpec((tm // ROWS, ROWS, tn), lambda n, i: (i, 0, n)),
                  pl.BlockSpec((tm // ROWS, 1, tn), lambda n, i: (i, 0, n))],
        out_specs=pl.BlockSpec((tm // ROWS, ROWS, tn), lambda n, i: (i, 0, n)),
        compiler_params=_cparams(2, 48),
        name="out_proj",
    )(a_p, a_s, w_bf16, x3, gate_e)


def _conv_prompt_kernel(*refs, tl, has_bias):
    if has_bias:
        x_ref, prev_ref, w_ref, b_ref, o_ref, xs_ref = refs
    else:
        x_ref, prev_ref, w_ref, o_ref, xs_ref = refs
    first = pl.program_id(1) == 0
    xs_ref[0:8, :] = jnp.where(first, 0.0, prev_ref[...])
    xs_ref[8:8 + tl, :] = x_ref[...]
    w = w_ref[...]
    y = w[3:4, :] * xs_ref[8:8 + tl, :]
    for kk in range(CONV_W - 1):
        y = y + w[kk:kk + 1, :] * xs_ref[5 + kk:5 + kk + tl, :]
    if has_bias:
        y = y + b_ref[...]
    o_ref[...] = _silu(y)


def _conv_prompt_call(proj, w_conv, bias, j, col0, width, batch, seq):
    tl, tc = 512, 2048
    lt = seq // tl
    has_bias = bias is not None
    in_specs = [pl.BlockSpec((tl, tc), lambda b, l, ct: (b * lt + l, ct + col0 // tc)),
                pl.BlockSpec((8, tc), lambda b, l, ct: (jnp.maximum((b * lt + l) * (tl // 8) - 1, 0),
                                                        ct + col0 // tc)),
                pl.BlockSpec((None, CONV_W, tc), lambda b, l, ct: (j, 0, ct))]
    args = [proj, proj, w_conv]
    if has_bias:
        in_specs.append(pl.BlockSpec((None, 1, tc), lambda b, l, ct: (j, 0, ct)))
        args.append(bias.reshape(bias.shape[0], 1, width))
    return pl.pallas_call(
        functools.partial(_conv_prompt_kernel, tl=tl, has_bias=has_bias),
        out_shape=jax.ShapeDtypeStruct((batch * seq, width), F32),
        grid=(batch, lt, width // tc),
        in_specs=in_specs,
        out_specs=pl.BlockSpec((tl, tc), lambda b, l, ct: (b * lt + l, ct)),
        scratch_shapes=[pltpu.VMEM((tl + 8, tc), F32)],
        compiler_params=_cparams(3, 40),
        name="conv_prompt",
    )(*args)


def _conv_sample_kernel(*refs, tb, seq, has_bias):
    if has_bias:
        x_ref, hist_ref, w_ref, b_ref, o_ref, xs_ref = refs
    else:
        x_ref, hist_ref, w_ref, o_ref, xs_ref = refs
    tc = x_ref.shape[1]
    xs_ref[:, 5:8, :] = hist_ref[...]
    xs_ref[:, 8:8 + seq, :] = x_ref[...].reshape(tb, seq, tc)
    w = w_ref[...]
    y = w[3:4, :] * xs_ref[:, 8:8 + seq, :]
    for kk in range(CONV_W - 1):
        y = y + w[kk:kk + 1, :] * xs_ref[:, 5 + kk:5 + kk + seq, :]
    if has_bias:
        y = y + b_ref[...]
    o_ref[...] = _silu(y).reshape(tb * seq, tc)


def _conv_sample_call(proj, hist, w_conv, bias, j, row0, col0, width, batch, seq):
    tb, tc = 32, 2048
    has_bias = bias is not None
    in_specs = [pl.BlockSpec((tb * seq, tc), lambda i, ct: (i + row0 // (tb * seq), ct + col0 // tc)),
                pl.BlockSpec((None, tb, CONV_W - 1, tc), lambda i, ct: (j, i, 0, ct)),
                pl.BlockSpec((None, CONV_W, tc), lambda i, ct: (j, 0, ct))]
    args = [proj, hist, w_conv]
    if has_bias:
        in_specs.append(pl.BlockSpec((None, 1, tc), lambda i, ct: (j, 0, ct)))
        args.append(bias.reshape(bias.shape[0], 1, width))
    return pl.pallas_call(
        functools.partial(_conv_sample_kernel, tb=tb, seq=seq, has_bias=has_bias),
        out_shape=jax.ShapeDtypeStruct((batch * seq, width), F32),
        grid=(batch // tb, width // tc),
        in_specs=in_specs,
        out_specs=pl.BlockSpec((tb * seq, tc), lambda i, ct: (i, ct)),
        scratch_shapes=[pltpu.VMEM((tb, 8 + seq, tc), F32)],
        compiler_params=_cparams(2, 40),
        name="conv_sample",
    )(*args)


def _dn_kernel(*refs, c, n_ch, hb, has_state):
    if has_state:
        (q_ref, k_ref, v_ref, z_ref, gcol_ref, grow_ref, prow_ref, pcol_ref, nw_ref, s0_ref,
         o_ref, s_ref, grc_ref) = refs
    else:
        (q_ref, k_ref, v_ref, z_ref, gcol_ref, grow_ref, prow_ref, pcol_ref, nw_ref,
         o_ref, s_ref, grc_ref) = refs
    hblk = pl.program_id(1)

    @pl.when(pl.program_id(2) == 0)
    def _():
        if has_state:
            s_ref[...] = s0_ref[...]
        else:
            s_ref[...] = jnp.zeros(s_ref.shape, F32)

    row = lax.broadcasted_iota(jnp.int32, (c, c), 0)
    col = lax.broadcasted_iota(jnp.int32, (c, c), 1)
    causal = row >= col
    strict = row > col
    tri = jnp.where(causal, 1.0, 0.0).astype(BF16)
    tri_t = jnp.where(row <= col, 1.0, 0.0).astype(BF16)
    lane = lax.broadcasted_iota(jnp.int32, (c, 128), 1)
    neg_a_row = -jnp.exp(prow_ref[0:1, :])
    dtb_row = prow_ref[1:2, :]
    pc = pcol_ref[...]
    neg_a_col = -jnp.exp(pc[:, 0:1])
    dtb_col = pc[:, 1:2]
    nw = nw_ref[...]
    nhalf = 64

    def chunk(ci, carry):
        r0 = pl.multiple_of(ci * c, c)
        gt = gcol_ref[pl.ds(r0, c), :]
        beta_all = jax.nn.sigmoid(gt)
        gcum_all = _cumsum_rows(tri, neg_a_row * _softplus(gt + dtb_row))
        grc_ref[...] = _cumsum_cols(neg_a_col * _softplus(grow_ref[ci] + dtb_col), tri_t)
        for hh in range(hb):
            hk = hblk * hb + hh
            q = q_ref[pl.ds(r0, c), hh * DN_DK:(hh + 1) * DN_DK]
            k = k_ref[pl.ds(r0, c), hh * DN_DK:(hh + 1) * DN_DK]
            qn = q * (lax.rsqrt(jnp.sum(q * q, axis=-1, keepdims=True) + EPS) * (DN_DK ** -0.5))
            kn = k * lax.rsqrt(jnp.sum(k * k, axis=-1, keepdims=True) + EPS)
            kq = jnp.concatenate([kn, qn], axis=0)
            kkqk = _dot_nt(kq, kn)
            kk = kkqk[:c]
            qk = kkqk[c:]
            gcr2 = grc_ref[pl.ds(nhalf + 2 * hk, 2), :]
            for e in range(2):
                vh = 2 * hk + e
                sl = slice((2 * hh + e) * DN_DV, (2 * hh + e + 1) * DN_DV)
                beta = jnp.sum(jnp.where(lane == vh, beta_all, 0.0), axis=1, keepdims=True)
                gcum = jnp.sum(jnp.where(lane == nhalf + vh, gcum_all, 0.0), axis=1, keepdims=True)
                diff = gcum - gcr2[e:e + 1, :]
                decay = jnp.where(causal, jnp.exp(jnp.where(causal, diff, 0.0)), 0.0)
                lm = jnp.where(strict, beta * kk * decay, 0.0)
                eg = jnp.exp(gcum)
                s = s_ref[2 * hh + e]
                kqs = _dot(kq * jnp.concatenate([eg, eg], axis=0), s)
                w = beta * (v_ref[pl.ds(r0, c), sl] - kqs[:c])
                u = _unit_lower_solve(lm, w, c, row, col)
                o = kqs[c:] + _dot(qk * decay, u)
                gl = gcum[c - 1:c, :]
                s_ref[2 * hh + e] = s * jnp.exp(gl) + _dot_tn(kn * jnp.exp(gl - gcum), u)
                on = o * lax.rsqrt(jnp.mean(o * o, axis=-1, keepdims=True) + EPS) * nw
                o_ref[pl.ds(r0, c), sl] = (on * _silu(z_ref[pl.ds(r0, c), sl])).astype(o_ref.dtype)
        return carry

    lax.fori_loop(0, n_ch, chunk, 0)


def _dn_call(qkv, proj, tail, grow_t, prow, pcol, nw, s0, j, *, batch, seq, c, tl, hb, row0,
             n_hk, out_dtype):
    lt = seq // tl
    key = n_hk * DN_DK
    z0 = 2 * key + 2 * key
    rb0 = row0 // tl
    has_state = s0 is not None

    def rows(b, l):
        return b * lt + l

    in_specs = [
        pl.BlockSpec((tl, hb * DN_DK), lambda b, h, l: (rows(b, l), h)),
        pl.BlockSpec((tl, hb * DN_DK), lambda b, h, l: (rows(b, l), h + key // (hb * DN_DK))),
        pl.BlockSpec((tl, 2 * hb * DN_DV), lambda b, h, l: (rows(b, l), h + 2 * key // (2 * hb * DN_DV))),
        pl.BlockSpec((tl, 2 * hb * DN_DV), lambda b, h, l: (rows(b, l) + rb0, h + z0 // (2 * hb * DN_DV))),
        pl.BlockSpec((tl, 128), lambda b, h, l: (rows(b, l) + rb0, 0)),
        pl.BlockSpec((None, tl // c, 128, c), lambda b, h, l: (b, l, 0, 0)),
        pl.BlockSpec((2, 128), lambda b, h, l: (0, 0)),
        pl.BlockSpec((128, 2), lambda b, h, l: (0, 0)),
        pl.BlockSpec((1, DN_DV), lambda b, h, l: (0, 0)),
    ]
    args = [qkv, qkv, qkv, proj, tail, grow_t, prow, pcol, nw]
    if has_state:
        in_specs.append(pl.BlockSpec((None, None, 2 * hb, DN_DK, DN_DV), lambda b, h, l: (j, b, h, 0, 0)))
        args.append(s0)
    return pl.pallas_call(
        functools.partial(_dn_kernel, c=c, n_ch=tl // c, hb=hb, has_state=has_state),
        out_shape=(jax.ShapeDtypeStruct((batch * seq, 2 * key), out_dtype),
                   jax.ShapeDtypeStruct((batch, 2 * n_hk, DN_DK, DN_DV), F32)),
        grid=(batch, n_hk // hb, lt),
        in_specs=in_specs,
        out_specs=(pl.BlockSpec((tl, 2 * hb * DN_DV), lambda b, h, l: (rows(b, l), h)),
                   pl.BlockSpec((None, 2 * hb, DN_DK, DN_DV), lambda b, h, l: (b, h, 0, 0))),
        scratch_shapes=[pltpu.VMEM((128, c), F32)],
        compiler_params=_cparams(3, 48),
        name="gated_delta",
    )(*args)


def _ssd_kernel(*refs, c, n_ch, gb, has_state):
    if has_state:
        (x_ref, b_ref, c_ref, z_ref, dcol_ref, drow_ref, prow_ref, pcol_ref, nw_ref, h0_ref,
         o_ref, h_ref) = refs
    else:
        (x_ref, b_ref, c_ref, z_ref, dcol_ref, drow_ref, prow_ref, pcol_ref, nw_ref,
         o_ref, h_ref) = refs
    r_heads = h_ref.shape[0] // gb
    inner = r_heads * SSM_P

    @pl.when(pl.program_id(2) == 0)
    def _():
        if has_state:
            h_ref[...] = h0_ref[...]
        else:
            h_ref[...] = jnp.zeros(h_ref.shape, F32)

    row = lax.broadcasted_iota(jnp.int32, (c, c), 0)
    col = lax.broadcasted_iota(jnp.int32, (c, c), 1)
    causal = row >= col
    tri = jnp.where(causal, 1.0, 0.0).astype(BF16)
    tri_t = jnp.where(row <= col, 1.0, 0.0).astype(BF16)

    def chunk(ci, carry):
        r0 = pl.multiple_of(ci * c, c)
        for gg in range(gb):
            pr = prow_ref[gg]
            pcc = pcol_ref[gg]
            dtc = _softplus(dcol_ref[gg, pl.ds(r0, c), :] + pr[1:2, :])
            cs = _cumsum_rows(tri, dtc * (-jnp.exp(pr[0:1, :])))
            dtr = _softplus(drow_ref[gg, ci] + pcc[:, 1:2])
            csr = _cumsum_cols(dtr * (-jnp.exp(pcc[:, 0:1])), tri_t)
            bc = b_ref[pl.ds(r0, c), gg * SSM_N:(gg + 1) * SSM_N]
            cc = c_ref[pl.ds(r0, c), gg * SSM_N:(gg + 1) * SSM_N]
            cbm = jnp.where(causal, _dot_nt(cc, bc), 0.0)
            h_all = h_ref[gg * r_heads:(gg + 1) * r_heads].reshape(inner, SSM_N)
            yoff = _dot_nt(cc, h_all)
            cl = cs[c - 1:c, :]
            ys, xws = [], []
            for r in range(r_heads):
                xr = x_ref[pl.ds(r0, c), gg * inner + r * SSM_P:gg * inner + (r + 1) * SSM_P]
                csc = cs[:, r:r + 1]
                diff = csc - csr[r:r + 1, :]
                seg = jnp.where(causal, jnp.exp(jnp.where(causal, diff, 0.0)), 0.0)
                dt_r = dtc[:, r:r + 1]
                y = (_dot(cbm * seg, xr * dt_r) + yoff[:, r * SSM_P:(r + 1) * SSM_P] * jnp.exp(csc)
                     + pr[2:3, r:r + 1] * xr)
                ys.append(y)
                xws.append(xr * (jnp.exp(cl[:, r:r + 1] - csc) * dt_r))
            upd = _dot_tn(jnp.concatenate(xws, axis=1), bc)
            for r in range(r_heads):
                hi = gg * r_heads + r
                h_ref[hi] = h_ref[hi] * jnp.exp(cl[:, r:r + 1]) + upd[r * SSM_P:(r + 1) * SSM_P]
            gsl = slice(gg * inner, (gg + 1) * inner)
            yz = jnp.concatenate(ys, axis=1) * _silu(z_ref[pl.ds(r0, c), gsl])
            on = yz * lax.rsqrt(jnp.mean(yz * yz, axis=-1, keepdims=True) + EPS) * nw_ref[:, gsl]
            o_ref[pl.ds(r0, c), gsl] = on.astype(o_ref.dtype)
        return carry

    lax.fori_loop(0, n_ch, chunk, 0)


def _ssd_call(xbc, proj, dcol, drow, prow, pcol, nw, h0, j, *, batch, seq, c, tl, gb, row0,
              n_heads, out_dtype):
    lt = seq // tl
    r_heads = n_heads // SSM_GROUPS
    ginner = r_heads * SSM_P
    inner = n_heads * SSM_P
    rb0 = row0 // tl
    has_state = h0 is not None

    def rows(b, l):
        return b * lt + l

    in_specs = [
        pl.BlockSpec((tl, gb * ginner), lambda b, g, l: (rows(b, l), g)),
        pl.BlockSpec((tl, gb * SSM_N), lambda b, g, l: (rows(b, l), g + inner // (gb * SSM_N))),
        pl.BlockSpec((tl, gb * SSM_N),
                     lambda b, g, l: (rows(b, l), g + (inner + SSM_GROUPS * SSM_N) // (gb * SSM_N))),
        pl.BlockSpec((tl, gb * ginner), lambda b, g, l: (rows(b, l) + rb0, g)),
        pl.BlockSpec((None, gb, tl, r_heads), lambda b, g, l: (b, g, l, 0)),
        pl.BlockSpec((None, gb, tl // c, r_heads, c), lambda b, g, l: (b, g, l, 0, 0)),
        pl.BlockSpec((gb, 3, r_heads), lambda b, g, l: (g, 0, 0)),
        pl.BlockSpec((gb, r_heads, 2), lambda b, g, l: (g, 0, 0)),
        pl.BlockSpec((1, gb * ginner), lambda b, g, l: (0, g)),
    ]
    args = [xbc, xbc, xbc, proj, dcol, drow, prow, pcol, nw]
    if has_state:
        in_specs.append(pl.BlockSpec((None, None, gb * r_heads, SSM_P, SSM_N),
                                     lambda b, g, l: (j, b, g, 0, 0)))
        args.append(h0)
    return pl.pallas_call(
        functools.partial(_ssd_kernel, c=c, n_ch=tl // c, gb=gb, has_state=has_state),
        out_shape=(jax.ShapeDtypeStruct((batch * seq, inner), out_dtype),
                   jax.ShapeDtypeStruct((batch, n_heads, SSM_P, SSM_N), F32)),
        grid=(batch, SSM_GROUPS // gb, lt),
        in_specs=in_specs,
        out_specs=(pl.BlockSpec((tl, gb * ginner), lambda b, g, l: (rows(b, l), g)),
                   pl.BlockSpec((None, gb * r_heads, SSM_P, SSM_N), lambda b, g, l: (b, g, 0, 0))),
        compiler_params=_cparams(3, 48),
        name="ssd",
    )(*args)


def kernel(x_prompt, x_sample, state_dn_S, state_dn_conv, state_ssm, state_ssm_conv, c_prompt, c_sample,
           w_ada, b_ada, norm_w, dn_w_in, dn_w_conv, dn_a_log, dn_dt_bias, dn_norm_w, dn_w_out,
           ssm_w_in, ssm_w_conv, ssm_conv_b, ssm_a_log, ssm_dt_bias, ssm_d, ssm_norm_w, ssm_w_out,
           final_norm_w):
    bp, lp, d = x_prompt.shape
    bs, ls, _ = x_sample.shape
    mp, ms = bp * lp, bs * ls
    m = mp + ms
    depth = w_ada.shape[0]
    n_hv = dn_a_log.shape[1]
    n_hk = n_hv // 2
    dn_key = n_hk * DN_DK
    dn_conv = 2 * dn_key + n_hv * DN_DV
    dn_main = dn_conv + n_hv * DN_DV
    n_sh = ssm_a_log.shape[1]
    r_heads = n_sh // SSM_GROUPS
    ssm_inner = n_sh * SSM_P
    ssm_conv = ssm_inner + 2 * SSM_GROUPS * SSM_N
    ssm_main = ssm_inner + ssm_conv
    cp = min(CHUNK, lp)
    cs_ = min(CHUNK, ls)
    assert lp % cp == 0 and ls == cs_ and ls % ROWS == 0 and lp % ROWS == 0
    assert ls >= CONV_W - 1 and lp >= CONV_W - 1 and 2 * n_hv == 128 and n_sh == 128

    x3 = jnp.concatenate([x_prompt.reshape(mp, d), x_sample.reshape(ms, d)], axis=0).reshape(m // ROWS, ROWS, d)

    nb = bp + bs
    pad = (-nb) % 16
    c_all = jnp.concatenate([c_prompt, c_sample, jnp.zeros((pad, d), F32)], axis=0)
    mod = _mod_call(c_all, w_ada, b_ada)

    def expand(t):
        return jnp.concatenate([jnp.repeat(t[:bp], lp // ROWS, axis=0),
                                jnp.repeat(t[bp:nb], ls // ROWS, axis=0)], axis=0).reshape(m // ROWS, 1, d)

    dn_s_p, dn_cv_p, ssm_h_p, ssm_cv_p = [], [], [], []
    dn_s_s, dn_cv_s, ssm_h_s, ssm_cv_s = [], [], [], []

    for i in range(depth):
        j = i // 2
        shift_e = expand(mod[i, :, :d])
        scale_e = expand(mod[i, :, d:2 * d])
        gate_e = expand(mod[i, :, 2 * d:])
        h = _prenorm_call(x3, norm_w[i], scale_e, shift_e)
        if i % 2 == 0:
            proj = _mm_in_call(h, dn_w_in, j, dn_main)
            tail, tail_t = _tail_call(h, dn_w_in, j, dn_main)
            qkv_p = _conv_prompt_call(proj, dn_w_conv, None, j, 0, dn_conv, bp, lp)
            qkv_s = _conv_sample_call(proj, state_dn_conv, dn_w_conv, None, j, mp, 0, dn_conv, bs, ls)
            grow_p = tail_t[:, :mp].reshape(128, bp, lp // cp, cp).transpose(1, 2, 0, 3)
            grow_s = tail_t[:, mp:].reshape(128, bs, 1, cs_).transpose(1, 2, 0, 3)
            prow = jnp.concatenate([jnp.zeros((2, n_hv), F32), jnp.stack([dn_a_log[j], dn_dt_bias[j]])], axis=1)
            pcol = prow.T
            nw = dn_norm_w[j].reshape(1, DN_DV)
            o_p, s_p = _dn_call(qkv_p, proj, tail, grow_p, prow, pcol, nw, None, j, batch=bp, seq=lp,
                                c=cp, tl=512, hb=2, row0=0, n_hk=n_hk, out_dtype=BF16)
            o_s, s_s = _dn_call(qkv_s, proj, tail, grow_s, prow, pcol, nw, state_dn_S, j, batch=bs, seq=ls,
                                c=cs_, tl=ls, hb=8, row0=mp, n_hk=n_hk, out_dtype=F32)
            dn_s_p.append(s_p)
            dn_s_s.append(s_s)
            dn_cv_p.append(proj[:mp].reshape(bp, lp, dn_main)[:, lp - (CONV_W - 1):, :dn_conv])
            dn_cv_s.append(proj[mp:].reshape(bs, ls, dn_main)[:, ls - (CONV_W - 1):, :dn_conv])
            w_out = dn_w_out[j].astype(BF16)
        else:
            proj = _mm_in_call(h, ssm_w_in, j, ssm_main)
            tail, tail_t = _tail_call(h, ssm_w_in, j, ssm_main)
            xbc_p = _conv_prompt_call(proj, ssm_w_conv, ssm_conv_b, j, ssm_inner, ssm_conv, bp, lp)
            xbc_s = _conv_sample_call(proj, state_ssm_conv, ssm_w_conv, ssm_conv_b, j, mp, ssm_inner,
                                      ssm_conv, bs, ls)
            dcol_p = tail[:mp].reshape(bp, lp, SSM_GROUPS, r_heads).transpose(0, 2, 1, 3)
            dcol_s = tail[mp:].reshape(bs, ls, SSM_GROUPS, r_heads).transpose(0, 2, 1, 3)
            drow_p = tail_t[:, :mp].reshape(SSM_GROUPS, r_heads, bp, lp // cp, cp).transpose(2, 0, 3, 1, 4)
            drow_s = tail_t[:, mp:].reshape(SSM_GROUPS, r_heads, bs, 1, cs_).transpose(2, 0, 3, 1, 4)
            prow = jnp.stack([ssm_a_log[j], ssm_dt_bias[j], ssm_d[j]]).reshape(3, SSM_GROUPS, r_heads)
            prow = prow.transpose(1, 0, 2)
            pcol = jnp.stack([ssm_a_log[j], ssm_dt_bias[j]], axis=-1).reshape(SSM_GROUPS, r_heads, 2)
            nw = ssm_norm_w[j].reshape(1, ssm_inner)
            o_p, s_p = _ssd_call(xbc_p, proj, dcol_p, drow_p, prow, pcol, nw, None, j, batch=bp, seq=lp,
                                 c=cp, tl=512, gb=1, row0=0, n_heads=n_sh, out_dtype=BF16)
            o_s, s_s = _ssd_call(xbc_s, proj, dcol_s, drow_s, prow, pcol, nw, state_ssm, j, batch=bs, seq=ls,
                                 c=cs_, tl=ls, gb=4, row0=mp, n_heads=n_sh, out_dtype=F32)
            ssm_h_p.append(s_p)
            ssm_h_s.append(s_s)
            ssm_cv_p.append(proj[:mp].reshape(bp, lp, ssm_main)[:, lp - (CONV_W - 1):, ssm_inner:])
            ssm_cv_s.append(proj[mp:].reshape(bs, ls, ssm_main)[:, ls - (CONV_W - 1):, ssm_inner:])
            w_out = ssm_w_out[j].astype(BF16)
        x3 = _mm_out_call(o_p, o_s.astype(BF16), w_out, x3, gate_e)

    x2 = x3.reshape(m, d)
    y_p = _final_norm_call(x2, final_norm_w, 0, mp).reshape(bp, lp, d)
    y_s = _final_norm_call(x2, final_norm_w, mp, ms).reshape(bs, ls, d)
    return (y_p, y_s, jnp.stack(dn_s_p), jnp.stack(dn_cv_p), jnp.stack(ssm_h_p), jnp.stack(ssm_cv_p),
            jnp.stack(dn_s_s), jnp.stack(dn_cv_s), jnp.stack(ssm_h_s), jnp.stack(ssm_cv_s))
```

```python
import functools

import jax
import jax.numpy as jnp
from jax import lax
from jax.experimental import pallas as pl
from jax.experimental.pallas import tpu as pltpu

F32 = jnp.float32
BF16 = jnp.bfloat16

EPS = 1e-6
CONV_W = 4
CHUNK = 64
DN_DK = 128
DN_DV = 128
SSM_P = 64
SSM_N = 128
SSM_GROUPS = 8
ROWS = 8
MIB = 1024 * 1024


def _cparams(n_axes, vmem_mib):
    return pltpu.CompilerParams(dimension_semantics=("arbitrary",) * n_axes,
                                vmem_limit_bytes=vmem_mib * MIB)


def _silu(x):
    return x * jax.nn.sigmoid(x)


def _softplus(x):
    return jnp.maximum(x, 0.0) + jnp.log1p(jnp.exp(-jnp.abs(x)))


def _mm(a, b):
    return jnp.dot(a, b, preferred_element_type=F32)


def _dot(a, b):
    return _mm(a.astype(BF16), b.astype(BF16))


def _dot_nt(a, b):
    return lax.dot_general(a.astype(BF16), b.astype(BF16), (((1,), (1,)), ((), ())),
                           preferred_element_type=F32)


def _dot_tn(a, b):
    return lax.dot_general(a.astype(BF16), b.astype(BF16), (((0,), (0,)), ((), ())),
                           preferred_element_type=F32)


def _split3(a):
    hi = a.astype(BF16)
    r = a - hi.astype(F32)
    mid = r.astype(BF16)
    lo = (r - mid.astype(F32)).astype(BF16)
    return hi, mid, lo


def _cumsum_rows(tri, g):
    hi, mid, lo = _split3(g)
    return _mm(tri, hi) + (_mm(tri, mid) + _mm(tri, lo))


def _cumsum_cols(g, tri_t):
    hi, mid, lo = _split3(g)
    return _mm(hi, tri_t) + (_mm(mid, tri_t) + _mm(lo, tri_t))


def _unit_lower_solve_many(lms, ws, c, row, col):
    bs = min(16, c)
    nb = c // bs
    eye = jnp.where(row == col, 1.0, 0.0).astype(F32)
    same = (row // bs) == (col // bs)
    ds = [jnp.where(same, lm, 0.0) for lm in lms]
    ps = [eye - d for d in ds]
    qs = [d.astype(BF16) for d in ds]
    n = 2
    while n < bs:
        qs = [_mm(q, q).astype(BF16) for q in qs]
        ps = [p + _mm(p.astype(BF16), q) for p, q in zip(ps, qs)]
        n *= 2
    pbs = [p.astype(BF16) for p in ps]
    ts = [_mm(p, w.astype(BF16)) for p, w in zip(pbs, ws)]
    if nb == 1:
        return ts
    ms = [_mm(p, jnp.where(same, 0.0, lm).astype(BF16)).astype(BF16) for p, lm in zip(pbs, lms)]
    ts = [t - _mm(m, t.astype(BF16)) for m, t in zip(ms, ts)]
    n = 2
    while n < nb:
        ms = [_mm(m, m).astype(BF16) for m in ms]
        ts = [t + _mm(m, t.astype(BF16)) for m, t in zip(ms, ts)]
        n *= 2
    return ts


def _mod_kernel(c_ref, w_ref, b_ref, o_ref):
    ca = _silu(c_ref[...]).astype(BF16)
    o_ref[...] = _mm(ca, w_ref[...].astype(BF16)) + b_ref[...]


def _mod_call(c_all, w_ada, b_ada):
    depth, d, n3 = w_ada.shape
    rows = c_all.shape[0]
    tn = 512
    return pl.pallas_call(
        _mod_kernel,
        out_shape=jax.ShapeDtypeStruct((depth, rows, n3), F32),
        grid=(depth, n3 // tn),
        in_specs=[pl.BlockSpec((rows, d), lambda i, n: (0, 0)),
                  pl.BlockSpec((None, d, tn), lambda i, n: (i, 0, n)),
                  pl.BlockSpec((None, 1, tn), lambda i, n: (i, 0, n))],
        out_specs=pl.BlockSpec((None, rows, tn), lambda i, n: (i, 0, n)),
        compiler_params=_cparams(2, 40),
        name="adaln_mod",
    )(c_all, w_ada, b_ada.reshape(depth, 1, n3))


def _prenorm_kernel(x_ref, nw_ref, sc_ref, sh_ref, o_ref):
    x = x_ref[...]
    y = x * lax.rsqrt(jnp.mean(x * x, axis=-1, keepdims=True) + EPS) * nw_ref[...]
    h = y * (1.0 + sc_ref[...]) + sh_ref[...]
    o_ref[...] = h.reshape(o_ref.shape).astype(BF16)


def _prenorm_call(x3, nw, scale_e, shift_e):
    mb, rows, d = x3.shape
    tb = 32
    return pl.pallas_call(
        _prenorm_kernel,
        out_shape=jax.ShapeDtypeStruct((mb * rows, d), BF16),
        grid=(mb // tb,),
        in_specs=[pl.BlockSpec((tb, rows, d), lambda t: (t, 0, 0)),
                  pl.BlockSpec((1, 1, d), lambda t: (0, 0, 0)),
                  pl.BlockSpec((tb, 1, d), lambda t: (t, 0, 0)),
                  pl.BlockSpec((tb, 1, d), lambda t: (t, 0, 0))],
        out_specs=pl.BlockSpec((tb * rows, d), lambda t: (t, 0)),
        compiler_params=_cparams(1, 40),
        name="prenorm",
    )(x3, nw.reshape(1, 1, d), scale_e, shift_e)


def _final_norm_kernel(x_ref, nw_ref, o_ref):
    x = x_ref[...]
    o_ref[...] = x * lax.rsqrt(jnp.mean(x * x, axis=-1, keepdims=True) + EPS) * nw_ref[...]


def _final_norm_call(x2, nw, row0, rows):
    d = x2.shape[1]
    tm = 512
    return pl.pallas_call(
        _final_norm_kernel,
        out_shape=jax.ShapeDtypeStruct((rows, d), F32),
        grid=(rows // tm,),
        in_specs=[pl.BlockSpec((tm, d), lambda t: (t + row0 // tm, 0)),
                  pl.BlockSpec((1, d), lambda t: (0, 0))],
        out_specs=pl.BlockSpec((tm, d), lambda t: (t, 0)),
        compiler_params=_cparams(1, 40),
        name="final_norm",
    )(x2, nw.reshape(1, d))


def _mm_in_kernel(h_ref, w_ref, o_ref, wb_ref):
    @pl.when(pl.program_id(1) == 0)
    def _():
        wb_ref[...] = w_ref[...].astype(BF16)

    o_ref[...] = _mm(h_ref[...], wb_ref[...])


def _mm_in_call(h, w, j, n_cols):
    m, k = h.shape
    tm, tn = 1024, 512
    return pl.pallas_call(
        _mm_in_kernel,
        out_shape=jax.ShapeDtypeStruct((m, n_cols), F32),
        grid=(n_cols // tn, m // tm),
        in_specs=[pl.BlockSpec((tm, k), lambda n, i: (i, 0)),
                  pl.BlockSpec((None, k, tn), lambda n, i: (j, 0, n))],
        out_specs=pl.BlockSpec((tm, tn), lambda n, i: (i, n)),
        scratch_shapes=[pltpu.VMEM((k, tn), BF16)],
        compiler_params=_cparams(2, 48),
        name="in_proj",
    )(h, w)


def _tail_kernel(h_ref, w_ref, wt_ref, o_ref, ot_ref):
    h = h_ref[...]
    o_ref[...] = _mm(h, w_ref[...].astype(BF16))
    ot_ref[...] = lax.dot_general(wt_ref[...].astype(BF16), h, (((1,), (1,)), ((), ())),
                                  preferred_element_type=F32)


def _tail_call(h, w, j, col0):
    m, k = h.shape
    tm = 1024
    wt = jnp.transpose(w[j, :, col0:col0 + 128])
    return pl.pallas_call(
        _tail_kernel,
        out_shape=(jax.ShapeDtypeStruct((m, 128), F32), jax.ShapeDtypeStruct((128, m), F32)),
        grid=(m // tm,),
        in_specs=[pl.BlockSpec((tm, k), lambda i: (i, 0)),
                  pl.BlockSpec((None, k, 128), lambda i: (j, 0, col0 // 128)),
                  pl.BlockSpec((128, k), lambda i: (0, 0))],
        out_specs=(pl.BlockSpec((tm, 128), lambda i: (i, 0)),
                   pl.BlockSpec((128, tm), lambda i: (0, i))),
        compiler_params=_cparams(1, 40),
        name="in_proj_tail",
    )(h, w, wt)


def _mm_out_kernel(ap_ref, as_ref, w_ref, x_ref, g_ref, o_ref, *, prompt_tiles):
    i = pl.program_id(1)

    def finish(acc):
        tm, tn = acc.shape
        o_ref[...] = x_ref[...] + g_ref[...] * acc.reshape(tm // ROWS, ROWS, tn)

    @pl.when(i < prompt_tiles)
    def _():
        finish(_mm(ap_ref[...], w_ref[...]))

    @pl.when(i >= prompt_tiles)
    def _():
        finish(_mm(as_ref[...], w_ref[...]))


def _mm_out_call(a_p, a_s, w_bf16, x3, gate_e):
    mp, k = a_p.shape
    ms = a_s.shape[0]
    d = w_bf16.shape[1]
    tm, tn = 256, 512
    pt, st = mp // tm, ms // tm
    return pl.pallas_call(
        functools.partial(_mm_out_kernel, prompt_tiles=pt),
        out_shape=jax.ShapeDtypeStruct(x3.shape, F32),
        grid=(d // tn, pt + st),
        in_specs=[pl.BlockSpec((tm, k), lambda n, i: (jnp.minimum(i, pt - 1), 0)),
                  pl.BlockSpec((tm, k), lambda n, i: (jnp.maximum(i - pt, 0), 0)),
                  pl.BlockSpec((k, tn), lambda n, i: (0, n)),
                  pl.BlockSpec((tm // ROWS, ROWS, tn), lambda n, i: (i, 0, n)),
                  pl.BlockSpec((tm // ROWS, 1, tn), lambda n, i: (i, 0, n))],
        out_specs=pl.BlockSpec((tm // ROWS, ROWS, tn), lambda n, i: (i, 0, n)),
        compiler_params=_cparams(2, 48),
        name="out_proj",
    )(a_p, a_s, w_bf16, x3, gate_e)


def _conv_prompt_kernel(*refs, tl, has_bias):
    if has_bias:
        x_ref, prev_ref, w_ref, b_ref, o_ref, xs_ref = refs
    else:
        x_ref, prev_ref, w_ref, o_ref, xs_ref = refs
    first = pl.program_id(1) == 0
    xs_ref[0:8, :] = jnp.where(first, 0.0, prev_ref[...])
    xs_ref[8:8 + tl, :] = x_ref[...]
    w = w_ref[...]
    y = w[3:4, :] * xs_ref[8:8 + tl, :]
    for kk in range(CONV_W - 1):
        y = y + w[kk:kk + 1, :] * xs_ref[5 + kk:5 + kk + tl, :]
    if has_bias:
        y = y + b_ref[...]
    o_ref[...] = _silu(y)


def _conv_prompt_call(proj, w_conv, bias, j, col0, width, batch, seq):
    tl, tc = 512, 2048
    lt = seq // tl
    has_bias = bias is not None
    in_specs = [pl.BlockSpec((tl, tc), lambda b, l, ct: (b * lt + l, ct + col0 // tc)),
                pl.BlockSpec((8, tc), lambda b, l, ct: (jnp.maximum((b * lt + l) * (tl // 8) - 1, 0),
                                                        ct + col0 // tc)),
                pl.BlockSpec((None, CONV_W, tc), lambda b, l, ct: (j, 0, ct))]
    args = [proj, proj, w_conv]
    if has_bias:
        in_specs.append(pl.BlockSpec((None, 1, tc), lambda b, l, ct: (j, 0, ct)))
        args.append(bias.reshape(bias.shape[0], 1, width))
    return pl.pallas_call(
        functools.partial(_conv_prompt_kernel, tl=tl, has_bias=has_bias),
        out_shape=jax.ShapeDtypeStruct((batch * seq, width), F32),
        grid=(batch, lt, width // tc),
        in_specs=in_specs,
        out_specs=pl.BlockSpec((tl, tc), lambda b, l, ct: (b * lt + l, ct)),
        scratch_shapes=[pltpu.VMEM((tl + 8, tc), F32)],
        compiler_params=_cparams(3, 40),
        name="conv_prompt",
    )(*args)


def _conv_sample_kernel(*refs, tb, seq, has_bias):
    if has_bias:
        x_ref, hist_ref, w_ref, b_ref, o_ref, xs_ref = refs
    else:
        x_ref, hist_ref, w_ref, o_ref, xs_ref = refs
    tc = x_ref.shape[1]
    xs_ref[:, 5:8, :] = hist_ref[...]
    xs_ref[:, 8:8 + seq, :] = x_ref[...].reshape(tb, seq, tc)
    w = w_ref[...]
    y = w[3:4, :] * xs_ref[:, 8:8 + seq, :]
    for kk in range(CONV_W - 1):
        y = y + w[kk:kk + 1, :] * xs_ref[:, 5 + kk:5 + kk + seq, :]
    if has_bias:
        y = y + b_ref[...]
    o_ref[...] = _silu(y).reshape(tb * seq, tc)


def _conv_sample_call(proj, hist, w_conv, bias, j, row0, col0, width, batch, seq):
    tb, tc = 32, 2048
    has_bias = bias is not None
    in_specs = [pl.BlockSpec((tb * seq, tc), lambda i, ct: (i + row0 // (tb * seq), ct + col0 // tc)),
                pl.BlockSpec((None, tb, CONV_W - 1, tc), lambda i, ct: (j, i, 0, ct)),
                pl.BlockSpec((None, CONV_W, tc), lambda i, ct: (j, 0, ct))]
    args = [proj, hist, w_conv]
    if has_bias:
        in_specs.append(pl.BlockSpec((None, 1, tc), lambda i, ct: (j, 0, ct)))
        args.append(bias.reshape(bias.shape[0], 1, width))
    return pl.pallas_call(
        functools.partial(_conv_sample_kernel, tb=tb, seq=seq, has_bias=has_bias),
        out_shape=jax.ShapeDtypeStruct((batch * seq, width), F32),
        grid=(batch // tb, width // tc),
        in_specs=in_specs,
        out_specs=pl.BlockSpec((tb * seq, tc), lambda i, ct: (i, ct)),
        scratch_shapes=[pltpu.VMEM((tb, 8 + seq, tc), F32)],
        compiler_params=_cparams(2, 40),
        name="conv_sample",
    )(*args)


def _fill_other_slots(stack_ref, slot):
    @pl.when(pl.program_id(2) == pl.num_programs(2) - 1)
    def _():
        for other in range(stack_ref.shape[0]):
            if other != slot:
                stack_ref[other] = stack_ref[slot]


def _dn_kernel(*refs, c, n_ch, hb, has_state, slot):
    q_ref, k_ref, v_ref, z_ref, gcol_ref, grow_ref, prow_ref, pcol_ref, nw_ref = refs[:9]
    s0_ref = refs[9] if has_state else None
    o_ref, sout_ref, grc_ref = refs[-3:]
    s_ref = sout_ref if slot is None else sout_ref.at[slot]
    hblk = pl.program_id(1)

    @pl.when(pl.program_id(2) == 0)
    def _():
        if has_state:
            s_ref[...] = s0_ref[...]
        else:
            s_ref[...] = jnp.zeros(s_ref.shape, F32)

    row = lax.broadcasted_iota(jnp.int32, (c, c), 0)
    col = lax.broadcasted_iota(jnp.int32, (c, c), 1)
    causal = row >= col
    strict = row > col
    tri = jnp.where(causal, 1.0, 0.0).astype(BF16)
    tri_t = jnp.where(row <= col, 1.0, 0.0).astype(BF16)
    lane = lax.broadcasted_iota(jnp.int32, (c, 128), 1)
    neg_a_row = -jnp.exp(prow_ref[0:1, :])
    dtb_row = prow_ref[1:2, :]
    pc = pcol_ref[...]
    neg_a_col = -jnp.exp(pc[:, 0:1])
    dtb_col = pc[:, 1:2]
    nw = nw_ref[...]
    nhalf = 64

    def chunk(ci, carry):
        r0 = pl.multiple_of(ci * c, c)
        gt = gcol_ref[pl.ds(r0, c), :]
        beta_all = jax.nn.sigmoid(gt)
        gcum_all = _cumsum_rows(tri, neg_a_row * _softplus(gt + dtb_row))
        grc_ref[...] = _cumsum_cols(neg_a_col * _softplus(grow_ref[ci] + dtb_col), tri_t)
        kns, kqs_in, kks, qks = [], [], [], []
        for hh in range(hb):
            q = q_ref[pl.ds(r0, c), hh * DN_DK:(hh + 1) * DN_DK]
            k = k_ref[pl.ds(r0, c), hh * DN_DK:(hh + 1) * DN_DK]
            qn = q * (lax.rsqrt(jnp.sum(q * q, axis=-1, keepdims=True) + EPS) * (DN_DK ** -0.5))
            kn = k * lax.rsqrt(jnp.sum(k * k, axis=-1, keepdims=True) + EPS)
            kq = jnp.concatenate([kn, qn], axis=0)
            kkqk = _dot_nt(kq, kn)
            kns.append(kn)
            kqs_in.append(kq)
            kks.append(kkqk[:c])
            qks.append(kkqk[c:])
        heads = [(hh, e) for hh in range(hb) for e in range(2)]
        betas, gcums, decays, lms, kqss = [], [], [], [], []
        for hh, e in heads:
            vh = 2 * (hblk * hb + hh) + e
            beta = jnp.sum(jnp.where(lane == vh, beta_all, 0.0), axis=1, keepdims=True)
            gcum = jnp.sum(jnp.where(lane == nhalf + vh, gcum_all, 0.0), axis=1, keepdims=True)
            diff = gcum - grc_ref[pl.ds(nhalf + vh, 1), :]
            decay = jnp.where(causal, jnp.exp(jnp.where(causal, diff, 0.0)), 0.0)
            eg = jnp.exp(gcum)
            betas.append(beta)
            gcums.append(gcum)
            decays.append(decay)
            lms.append(jnp.where(strict, beta * kks[hh] * decay, 0.0))
            kqss.append(_dot(kqs_in[hh] * jnp.concatenate([eg, eg], axis=0), s_ref[2 * hh + e]))
        ws = [betas[i] * (v_ref[pl.ds(r0, c), (2 * hh + e) * DN_DV:(2 * hh + e + 1) * DN_DV] - kqss[i][:c])
              for i, (hh, e) in enumerate(heads)]
        us = _unit_lower_solve_many(lms, ws, c, row, col)
        for i, (hh, e) in enumerate(heads):
            sl = slice((2 * hh + e) * DN_DV, (2 * hh + e + 1) * DN_DV)
            ub = us[i].astype(BF16)
            o = kqss[i][c:] + _mm((qks[hh] * decays[i]).astype(BF16), ub)
            gl = gcums[i][c - 1:c, :]
            s_ref[2 * hh + e] = (s_ref[2 * hh + e] * jnp.exp(gl)
                                 + _dot_tn(kns[hh] * jnp.exp(gl - gcums[i]), ub))
            on = o * lax.rsqrt(jnp.mean(o * o, axis=-1, keepdims=True) + EPS) * nw
            o_ref[pl.ds(r0, c), sl] = (on * _silu(z_ref[pl.ds(r0, c), sl])).astype(o_ref.dtype)
        return carry

    lax.fori_loop(0, n_ch, chunk, 0)
    if slot is not None:
        _fill_other_slots(sout_ref, slot)


def _dn_call(qkv, proj, tail, grow_t, prow, pcol, nw, s0, stack, j, *, batch, seq, c, tl, hb, row0,
             n_hk, n_layers, out_dtype):
    lt = seq // tl
    key = n_hk * DN_DK
    z0 = 2 * key + 2 * key
    rb0 = row0 // tl
    has_state = s0 is not None

    def rows(b, l):
        return b * lt + l

    in_specs = [
        pl.BlockSpec((tl, hb * DN_DK), lambda b, h, l: (rows(b, l), h)),
        pl.BlockSpec((tl, hb * DN_DK), lambda b, h, l: (rows(b, l), h + key // (hb * DN_DK))),
        pl.BlockSpec((tl, 2 * hb * DN_DV), lambda b, h, l: (rows(b, l), h + 2 * key // (2 * hb * DN_DV))),
        pl.BlockSpec((tl, 2 * hb * DN_DV), lambda b, h, l: (rows(b, l) + rb0, h + z0 // (2 * hb * DN_DV))),
        pl.BlockSpec((tl, 128), lambda b, h, l: (rows(b, l) + rb0, 0)),
        pl.BlockSpec((None, tl // c, 128, c), lambda b, h, l: (b, l, 0, 0)),
        pl.BlockSpec((2, 128), lambda b, h, l: (0, 0)),
        pl.BlockSpec((128, 2), lambda b, h, l: (0, 0)),
        pl.BlockSpec((1, DN_DV), lambda b, h, l: (0, 0)),
    ]
    args = [qkv, qkv, qkv, proj, tail, grow_t, prow, pcol, nw]
    if has_state:
        in_specs.append(pl.BlockSpec((None, None, 2 * hb, DN_DK, DN_DV), lambda b, h, l: (j, b, h, 0, 0)))
        args.append(s0)
    if stack is None:
        aliases, slot = {}, j
        s_spec = pl.BlockSpec((n_layers, None, 2 * hb, DN_DK, DN_DV), lambda b, h, l: (0, b, h, 0, 0))
    else:
        aliases, slot = {len(args): 1}, None
        s_spec = pl.BlockSpec((None, None, 2 * hb, DN_DK, DN_DV), lambda b, h, l: (j, b, h, 0, 0))
        in_specs.append(pl.BlockSpec(memory_space=pl.ANY))
        args.append(stack)
    return pl.pallas_call(
        functools.partial(_dn_kernel, c=c, n_ch=tl // c, hb=hb, has_state=has_state, slot=slot),
        out_shape=(jax.ShapeDtypeStruct((batch * seq, 2 * key), out_dtype),
                   jax.ShapeDtypeStruct((n_layers, batch, 2 * n_hk, DN_DK, DN_DV), F32)),
        grid=(batch, n_hk // hb, lt),
        in_specs=in_specs,
        out_specs=(pl.BlockSpec((tl, 2 * hb * DN_DV), lambda b, h, l: (rows(b, l), h)), s_spec),
        scratch_shapes=[pltpu.VMEM((128, c), F32)],
        input_output_aliases=aliases,
        compiler_params=_cparams(3, 56),
        name="gated_delta",
    )(*args)


def _ssd_kernel(*refs, c, n_ch, gb, has_state, slot):
    x_ref, b_ref, c_ref, z_ref, dcol_ref, drow_ref, prow_ref, pcol_ref, dx_ref, nw_ref = refs[:10]
    h0_ref = refs[10] if has_state else None
    o_ref, hout_ref = refs[-2:]
    h_ref = hout_ref if slot is None else hout_ref.at[slot]
    r_heads = h_ref.shape[0] // gb
    n_pairs = r_heads // 2
    inner = r_heads * SSM_P
    pw = 2 * SSM_P
    c2 = 2 * c

    @pl.when(pl.program_id(2) == 0)
    def _():
        if has_state:
            h_ref[...] = h0_ref[...]
        else:
            h_ref[...] = jnp.zeros(h_ref.shape, F32)

    row = lax.broadcasted_iota(jnp.int32, (c, c), 0)
    col = lax.broadcasted_iota(jnp.int32, (c, c), 1)
    causal = row >= col
    tri = jnp.where(causal, 1.0, 0.0).astype(BF16)
    row2 = lax.broadcasted_iota(jnp.int32, (c2, c2), 0)
    col2 = lax.broadcasted_iota(jnp.int32, (c2, c2), 1)
    tri2_t = jnp.where((row2 <= col2) & ((row2 // c) == (col2 // c)), 1.0, 0.0).astype(BF16)
    prow_i = lax.broadcasted_iota(jnp.int32, (c, c2), 0)
    pcol_i = lax.broadcasted_iota(jnp.int32, (c, c2), 1)
    causal2 = prow_i >= (pcol_i % c)
    eh = lax.broadcasted_iota(jnp.int32, (r_heads, inner), 0)
    el = lax.broadcasted_iota(jnp.int32, (r_heads, inner), 1)
    spread_x = jnp.where(el // SSM_P == eh, 1.0, 0.0).astype(BF16)
    if c == SSM_P:
        spread_m = spread_x
    else:
        eh2 = lax.broadcasted_iota(jnp.int32, (r_heads, r_heads * c), 0)
        el2 = lax.broadcasted_iota(jnp.int32, (r_heads, r_heads * c), 1)
        spread_m = jnp.where(el2 // c == eh2, 1.0, 0.0).astype(BF16)
    lane_p = lax.broadcasted_iota(jnp.int32, (c, pw), 1)
    left = lane_p < SSM_P

    def chunk(ci, carry):
        r0 = pl.multiple_of(ci * c, c)
        for gg in range(gb):
            pr = prow_ref[gg]
            dtc = _softplus(dcol_ref[gg, pl.ds(r0, c), :] + pr[1:2, :])
            cs = _cumsum_rows(tri, dtc * (-jnp.exp(pr[0:1, :])))
            cl = cs[c - 1:c, :]
            wgt = jnp.exp(cl - cs) * dtc
            hi, mid, lo = _split3(cs)
            cs3 = jnp.concatenate([hi.astype(F32), mid.astype(F32), lo.astype(F32)], axis=0)
            oth = jnp.concatenate([dtc, wgt, jnp.exp(cs)], axis=0)
            if c == SSM_P:
                ex = _mm(jnp.concatenate([cs3, oth], axis=0).astype(BF16), spread_x)
                cs_m = ex[0:c] + (ex[c:2 * c] + ex[2 * c:3 * c])
                ex = ex[3 * c:]
            else:
                exm = _mm(cs3.astype(BF16), spread_m)
                cs_m = exm[0:c] + (exm[c:2 * c] + exm[2 * c:3 * c])
                ex = _mm(oth.astype(BF16), spread_x)
            dt_x, w_x, ecs_x = ex[0:c], ex[c:2 * c], ex[2 * c:3 * c]
            pcr = pcol_ref[gg]
            dtr = _softplus(drow_ref[gg, ci] + pcr[1])
            csr = _cumsum_cols(dtr * (-jnp.exp(pcr[0])), tri2_t)
            bc = b_ref[pl.ds(r0, c), gg * SSM_N:(gg + 1) * SSM_N]
            cc = c_ref[pl.ds(r0, c), gg * SSM_N:(gg + 1) * SSM_N]
            cbm = jnp.where(causal, _dot_nt(cc, bc), 0.0)
            cb2 = jnp.concatenate([cbm, cbm], axis=1)
            h_all = h_ref[gg * r_heads:(gg + 1) * r_heads].reshape(inner, SSM_N)
            yoff = _dot_nt(cc, h_all)
            gsl = slice(gg * inner, (gg + 1) * inner)
            x_all = x_ref[pl.ds(r0, c), gsl]
            xdt = x_all * dt_x
            ys = []
            for p in range(n_pairs):
                diff = cs_m[:, p * c2:(p + 1) * c2] - csr[p:p + 1, :]
                seg = jnp.where(causal2, jnp.exp(jnp.where(causal2, diff, 0.0)), 0.0)
                xp = xdt[:, p * pw:(p + 1) * pw]
                rhs = jnp.concatenate([jnp.where(left, xp, 0.0), jnp.where(left, 0.0, xp)], axis=0)
                ys.append(_dot(cb2 * seg, rhs))
            y = jnp.concatenate(ys, axis=1) + yoff * ecs_x + dx_ref[gg] * x_all
            upd = _dot_tn(x_all * w_x, bc)
            for r in range(r_heads):
                hd = gg * r_heads + r
                h_ref[hd] = h_ref[hd] * jnp.exp(cl[:, r:r + 1]) + upd[r * SSM_P:(r + 1) * SSM_P]
            yz = y * _silu(z_ref[pl.ds(r0, c), gsl])
            on = yz * lax.rsqrt(jnp.mean(yz * yz, axis=-1, keepdims=True) + EPS) * nw_ref[:, gsl]
            o_ref[pl.ds(r0, c), gsl] = on.astype(o_ref.dtype)
        return carry

    lax.fori_loop(0, n_ch, chunk, 0)
    if slot is not None:
        _fill_other_slots(hout_ref, slot)


def _ssd_call(xbc, proj, dcol, drow, prow, pcol, dexp, nw, h0, stack, j, *, batch, seq, c, tl, gb, row0,
              n_heads, n_layers, out_dtype):
    lt = seq // tl
    r_heads = n_heads // SSM_GROUPS
    ginner = r_heads * SSM_P
    inner = n_heads * SSM_P
    rb0 = row0 // tl
    has_state = h0 is not None

    def rows(b, l):
        return b * lt + l

    in_specs = [
        pl.BlockSpec((tl, gb * ginner), lambda b, g, l: (rows(b, l), g)),
        pl.BlockSpec((tl, gb * SSM_N), lambda b, g, l: (rows(b, l), g + inner // (gb * SSM_N))),
        pl.BlockSpec((tl, gb * SSM_N),
                     lambda b, g, l: (rows(b, l), g + (inner + SSM_GROUPS * SSM_N) // (gb * SSM_N))),
        pl.BlockSpec((tl, gb * ginner), lambda b, g, l: (rows(b, l) + rb0, g)),
        pl.BlockSpec((None, gb, tl, r_heads), lambda b, g, l: (b, g, l, 0)),
        pl.BlockSpec((None, gb, tl // c, r_heads // 2, 2 * c), lambda b, g, l: (b, g, l, 0, 0)),
        pl.BlockSpec((gb, 2, r_heads), lambda b, g, l: (g, 0, 0)),
        pl.BlockSpec((gb, 2, r_heads // 2, 2 * c), lambda b, g, l: (g, 0, 0, 0)),
        pl.BlockSpec((gb, 1, ginner), lambda b, g, l: (g, 0, 0)),
        pl.BlockSpec((1, gb * ginner), lambda b, g, l: (0, g)),
    ]
    args = [xbc, xbc, xbc, proj, dcol, drow, prow, pcol, dexp, nw]
    if has_state:
        in_specs.append(pl.BlockSpec((None, None, gb * r_heads, SSM_P, SSM_N),
                                     lambda b, g, l: (j, b, g, 0, 0)))
        args.append(h0)
    if stack is None:
        aliases, slot = {}, j
        h_spec = pl.BlockSpec((n_layers, None, gb * r_heads, SSM_P, SSM_N), lambda b, g, l: (0, b, g, 0, 0))
    else:
        aliases, slot = {len(args): 1}, None
        h_spec = pl.BlockSpec((None, None, gb * r_heads, SSM_P, SSM_N), lambda b, g, l: (j, b, g, 0, 0))
        in_specs.append(pl.BlockSpec(memory_space=pl.ANY))
        args.append(stack)
    return pl.pallas_call(
        functools.partial(_ssd_kernel, c=c, n_ch=tl // c, gb=gb, has_state=has_state, slot=slot),
        out_shape=(jax.ShapeDtypeStruct((batch * seq, inner), out_dtype),
                   jax.ShapeDtypeStruct((n_layers, batch, n_heads, SSM_P, SSM_N), F32)),
        grid=(batch, SSM_GROUPS // gb, lt),
        in_specs=in_specs,
        out_specs=(pl.BlockSpec((tl, gb * ginner), lambda b, g, l: (rows(b, l), g)), h_spec),
        input_output_aliases=aliases,
        compiler_params=_cparams(3, 48),
        name="ssd",
    )(*args)


def kernel(x_prompt, x_sample, state_dn_S, state_dn_conv, state_ssm, state_ssm_conv, c_prompt, c_sample,
           w_ada, b_ada, norm_w, dn_w_in, dn_w_conv, dn_a_log, dn_dt_bias, dn_norm_w, dn_w_out,
           ssm_w_in, ssm_w_conv, ssm_conv_b, ssm_a_log, ssm_dt_bias, ssm_d, ssm_norm_w, ssm_w_out,
           final_norm_w):
    bp, lp, d = x_prompt.shape
    bs, ls, _ = x_sample.shape
    mp, ms = bp * lp, bs * ls
    m = mp + ms
    depth = w_ada.shape[0]
    n_hv = dn_a_log.shape[1]
    n_hk = n_hv // 2
    dn_key = n_hk * DN_DK
    dn_conv = 2 * dn_key + n_hv * DN_DV
    dn_main = dn_conv + n_hv * DN_DV
    n_sh = ssm_a_log.shape[1]
    r_heads = n_sh // SSM_GROUPS
    ssm_inner = n_sh * SSM_P
    ssm_conv = ssm_inner + 2 * SSM_GROUPS * SSM_N
    ssm_main = ssm_inner + ssm_conv
    cp = min(CHUNK, lp)
    cs_ = min(CHUNK, ls)
    assert lp % cp == 0 and ls == cs_ and ls % ROWS == 0 and lp % ROWS == 0
    assert ls >= CONV_W - 1 and lp >= CONV_W - 1 and 2 * n_hv == 128 and n_sh == 128

    x3 = jnp.concatenate([x_prompt.reshape(mp, d), x_sample.reshape(ms, d)], axis=0).reshape(m // ROWS, ROWS, d)

    nb = bp + bs
    pad = (-nb) % 16
    c_all = jnp.concatenate([c_prompt, c_sample, jnp.zeros((pad, d), F32)], axis=0)
    mod = _mod_call(c_all, w_ada, b_ada)

    def expand(t):
        return jnp.concatenate([jnp.repeat(t[:bp], lp // ROWS, axis=0),
                                jnp.repeat(t[bp:nb], ls // ROWS, axis=0)], axis=0).reshape(m // ROWS, 1, d)

    n_dn = dn_w_in.shape[0]
    n_ssm = ssm_w_in.shape[0]
    kw = CONV_W - 1
    dn_cv_p, ssm_cv_p, dn_cv_s, ssm_cv_s = [], [], [], []
    dn_s_p = dn_s_s = ssm_h_p = ssm_h_s = None

    def last_rows(proj, col0, col1):
        p = jnp.stack([lax.slice(proj, (b * lp + lp - kw, col0), ((b + 1) * lp, col1)) for b in range(bp)])
        s = lax.slice(proj, (mp, col0), (m, col1)).reshape(bs, ls, col1 - col0)[:, ls - kw:]
        return p, s

    for i in range(depth):
        j = i // 2
        shift_e = expand(mod[i, :, :d])
        scale_e = expand(mod[i, :, d:2 * d])
        gate_e = expand(mod[i, :, 2 * d:])
        h = _prenorm_call(x3, norm_w[i], scale_e, shift_e)
        if i % 2 == 0:
            proj = _mm_in_call(h, dn_w_in, j, dn_main)
            tail, tail_t = _tail_call(h, dn_w_in, j, dn_main)
            qkv_p = _conv_prompt_call(proj, dn_w_conv, None, j, 0, dn_conv, bp, lp)
            qkv_s = _conv_sample_call(proj, state_dn_conv, dn_w_conv, None, j, mp, 0, dn_conv, bs, ls)
            grow_p = tail_t[:, :mp].reshape(128, bp, lp // cp, cp).transpose(1, 2, 0, 3)
            grow_s = tail_t[:, mp:].reshape(128, bs, 1, cs_).transpose(1, 2, 0, 3)
            prow = jnp.concatenate([jnp.zeros((2, n_hv), F32), jnp.stack([dn_a_log[j], dn_dt_bias[j]])], axis=1)
            pcol = prow.T
            nw = dn_norm_w[j].reshape(1, DN_DV)
            o_p, dn_s_p = _dn_call(qkv_p, proj, tail, grow_p, prow, pcol, nw, None, dn_s_p, j, batch=bp,
                                   seq=lp, c=cp, tl=512, hb=8, row0=0, n_hk=n_hk, n_layers=n_dn,
                                   out_dtype=BF16)
            o_s, dn_s_s = _dn_call(qkv_s, proj, tail, grow_s, prow, pcol, nw, state_dn_S, dn_s_s, j, batch=bs,
                                   seq=ls, c=cs_, tl=ls, hb=16, row0=mp, n_hk=n_hk, n_layers=n_dn,
                                   out_dtype=F32)
            cv_p, cv_s = last_rows(proj, 0, dn_conv)
            dn_cv_p.append(cv_p)
            dn_cv_s.append(cv_s)
            w_out = dn_w_out[j].astype(BF16)
        else:
            proj = _mm_in_call(h, ssm_w_in, j, ssm_main)
            tail, tail_t = _tail_call(h, ssm_w_in, j, ssm_main)
            xbc_p = _conv_prompt_call(proj, ssm_w_conv, ssm_conv_b, j, ssm_inner, ssm_conv, bp, lp)
            xbc_s = _conv_sample_call(proj, state_ssm_conv, ssm_w_conv, ssm_conv_b, j, mp, ssm_inner,
                                      ssm_conv, bs, ls)
            npair = r_heads // 2
            dcol_p = tail[:mp].reshape(bp, lp, SSM_GROUPS, r_heads).transpose(0, 2, 1, 3)
            dcol_s = tail[mp:].reshape(bs, ls, SSM_GROUPS, r_heads).transpose(0, 2, 1, 3)
            drow_p = tail_t[:, :mp].reshape(SSM_GROUPS, npair, 2, bp, lp // cp, cp).transpose(3, 0, 4, 1, 2, 5)
            drow_p = drow_p.reshape(bp, SSM_GROUPS, lp // cp, npair, 2 * cp)
            drow_s = tail_t[:, mp:].reshape(SSM_GROUPS, npair, 2, bs, 1, cs_).transpose(3, 0, 4, 1, 2, 5)
            drow_s = drow_s.reshape(bs, SSM_GROUPS, 1, npair, 2 * cs_)
            par = jnp.stack([ssm_a_log[j], ssm_dt_bias[j]])
            prow = par.reshape(2, SSM_GROUPS, r_heads).transpose(1, 0, 2)
            par_pairs = par.reshape(2, SSM_GROUPS, npair, 2, 1).transpose(1, 0, 2, 3, 4)
            pcol_p = jnp.broadcast_to(par_pairs, (SSM_GROUPS, 2, npair, 2, cp)).reshape(SSM_GROUPS, 2, npair, 2 * cp)
            pcol_s = jnp.broadcast_to(par_pairs, (SSM_GROUPS, 2, npair, 2, cs_)).reshape(SSM_GROUPS, 2, npair, 2 * cs_)
            dexp = jnp.repeat(ssm_d[j], SSM_P).reshape(SSM_GROUPS, 1, r_heads * SSM_P)
            nw = ssm_norm_w[j].reshape(1, ssm_inner)
            o_p, ssm_h_p = _ssd_call(xbc_p, proj, dcol_p, drow_p, prow, pcol_p, dexp, nw, None, ssm_h_p, j,
                                     batch=bp, seq=lp, c=cp, tl=512, gb=1, row0=0, n_heads=n_sh,
                                     n_layers=n_ssm, out_dtype=BF16)
            o_s, ssm_h_s = _ssd_call(xbc_s, proj, dcol_s, drow_s, prow, pcol_s, dexp, nw, state_ssm, ssm_h_s, j,
                                     batch=bs, seq=ls, c=cs_, tl=ls, gb=4, row0=mp, n_heads=n_sh,
                                     n_layers=n_ssm, out_dtype=F32)
            cv_p, cv_s = last_rows(proj, ssm_inner, ssm_main)
            ssm_cv_p.append(cv_p)
            ssm_cv_s.append(cv_s)
            w_out = ssm_w_out[j].astype(BF16)
        x3 = _mm_out_call(o_p, o_s.astype(BF16), w_out, x3, gate_e)

    x2 = x3.reshape(m, d)
    y_p = _final_norm_call(x2, final_norm_w, 0, mp).reshape(bp, lp, d)
    y_s = _final_norm_call(x2, final_norm_w, mp, ms).reshape(bs, ls, d)
    return (y_p, y_s, dn_s_p, jnp.stack(dn_cv_p), ssm_h_p, jnp.stack(ssm_cv_p),
            dn_s_s, jnp.stack(dn_cv_s), ssm_h_s, jnp.stack(ssm_cv_s))
```

```python
import functools

import jax
import jax.numpy as jnp
from jax import lax
from jax.experimental import pallas as pl
from jax.experimental.pallas import tpu as pltpu

F32 = jnp.float32
BF16 = jnp.bfloat16

EPS = 1e-6
CONV_W = 4
CHUNK = 64
DN_DK = 128
DN_DV = 128
SSM_P = 64
SSM_N = 128
SSM_GROUPS = 8
ROWS = 8
MIB = 1024 * 1024


def _cparams(n_axes, vmem_mib):
    return pltpu.CompilerParams(dimension_semantics=("arbitrary",) * n_axes,
                                vmem_limit_bytes=vmem_mib * MIB)


def _silu(x):
    return x * jax.nn.sigmoid(x)


def _softplus(x):
    return jnp.maximum(x, 0.0) + jnp.log1p(jnp.exp(-jnp.abs(x)))


def _mm(a, b):
    return jnp.dot(a, b, preferred_element_type=F32)


def _dot(a, b):
    return _mm(a.astype(BF16), b.astype(BF16))


def _dot_nt(a, b):
    return lax.dot_general(a.astype(BF16), b.astype(BF16), (((1,), (1,)), ((), ())),
                           preferred_element_type=F32)


def _dot_tn(a, b):
    return lax.dot_general(a.astype(BF16), b.astype(BF16), (((0,), (0,)), ((), ())),
                           preferred_element_type=F32)


def _split3(a):
    hi = a.astype(BF16)
    r = a - hi.astype(F32)
    mid = r.astype(BF16)
    lo = (r - mid.astype(F32)).astype(BF16)
    return hi, mid, lo


def _cumsum_rows(tri, g):
    hi, mid, lo = _split3(g)
    return _mm(tri, hi) + (_mm(tri, mid) + _mm(tri, lo))


def _cumsum_cols(g, tri_t):
    hi, mid, lo = _split3(g)
    return _mm(hi, tri_t) + (_mm(mid, tri_t) + _mm(lo, tri_t))


def _unit_lower_solve_many(lms, ws, c, row, col):
    bs = min(16, c)
    nb = c // bs
    eye = jnp.where(row == col, 1.0, 0.0).astype(F32)
    same = (row // bs) == (col // bs)
    ds = [jnp.where(same, lm, 0.0) for lm in lms]
    ps = [eye - d for d in ds]
    qs = [d.astype(BF16) for d in ds]
    n = 2
    while n < bs:
        qs = [_mm(q, q).astype(BF16) for q in qs]
        ps = [p + _mm(p.astype(BF16), q) for p, q in zip(ps, qs)]
        n *= 2
    pbs = [p.astype(BF16) for p in ps]
    ts = [_mm(p, w.astype(BF16)) for p, w in zip(pbs, ws)]
    if nb == 1:
        return ts
    ms = [_mm(p, jnp.where(same, 0.0, lm).astype(BF16)).astype(BF16) for p, lm in zip(pbs, lms)]
    ts = [t - _mm(m, t.astype(BF16)) for m, t in zip(ms, ts)]
    n = 2
    while n < nb:
        ms = [_mm(m, m).astype(BF16) for m in ms]
        ts = [t + _mm(m, t.astype(BF16)) for m, t in zip(ms, ts)]
        n *= 2
    return ts


def _mod_kernel(c_ref, w_ref, b_ref, o_ref):
    ca = _silu(c_ref[...]).astype(BF16)
    o_ref[...] = _mm(ca, w_ref[...].astype(BF16)) + b_ref[...]


def _mod_call(c_all, w_ada, b_ada):
    depth, d, n3 = w_ada.shape
    rows = c_all.shape[0]
    tn = 512
    return pl.pallas_call(
        _mod_kernel,
        out_shape=jax.ShapeDtypeStruct((depth, rows, n3), F32),
        grid=(depth, n3 // tn),
        in_specs=[pl.BlockSpec((rows, d), lambda i, n: (0, 0)),
                  pl.BlockSpec((None, d, tn), lambda i, n: (i, 0, n)),
                  pl.BlockSpec((None, 1, tn), lambda i, n: (i, 0, n))],
        out_specs=pl.BlockSpec((None, rows, tn), lambda i, n: (i, 0, n)),
        compiler_params=_cparams(2, 40),
        name="adaln_mod",
    )(c_all, w_ada, b_ada.reshape(depth, 1, n3))


def _prenorm_kernel(x_ref, nw_ref, sc_ref, sh_ref, o_ref):
    x = x_ref[...]
    y = x * lax.rsqrt(jnp.mean(x * x, axis=-1, keepdims=True) + EPS) * nw_ref[...]
    h = y * (1.0 + sc_ref[...]) + sh_ref[...]
    o_ref[...] = h.reshape(o_ref.shape).astype(BF16)


def _prenorm_call(x3, nw, scale_e, shift_e):
    mb, rows, d = x3.shape
    tb = 32
    return pl.pallas_call(
        _prenorm_kernel,
        out_shape=jax.ShapeDtypeStruct((mb * rows, d), BF16),
        grid=(mb // tb,),
        in_specs=[pl.BlockSpec((tb, rows, d), lambda t: (t, 0, 0)),
                  pl.BlockSpec((1, 1, d), lambda t: (0, 0, 0)),
                  pl.BlockSpec((tb, 1, d), lambda t: (t, 0, 0)),
                  pl.BlockSpec((tb, 1, d), lambda t: (t, 0, 0))],
        out_specs=pl.BlockSpec((tb * rows, d), lambda t: (t, 0)),
        compiler_params=_cparams(1, 40),
        name="prenorm",
    )(x3, nw.reshape(1, 1, d), scale_e, shift_e)


def _final_norm_kernel(x_ref, nw_ref, o_ref):
    x = x_ref[...]
    o_ref[...] = x * lax.rsqrt(jnp.mean(x * x, axis=-1, keepdims=True) + EPS) * nw_ref[...]


def _final_norm_call(x2, nw, row0, rows):
    d = x2.shape[1]
    tm = 512
    return pl.pallas_call(
        _final_norm_kernel,
        out_shape=jax.ShapeDtypeStruct((rows, d), F32),
        grid=(rows // tm,),
        in_specs=[pl.BlockSpec((tm, d), lambda t: (t + row0 // tm, 0)),
                  pl.BlockSpec((1, d), lambda t: (0, 0))],
        out_specs=pl.BlockSpec((tm, d), lambda t: (t, 0)),
        compiler_params=_cparams(1, 40),
        name="final_norm",
    )(x2, nw.reshape(1, d))


def _mm_in_kernel(h_ref, w_ref, o_ref, wb_ref):
    @pl.when(pl.program_id(1) == 0)
    def _():
        wb_ref[...] = w_ref[...].astype(BF16)

    o_ref[...] = _mm(h_ref[...], wb_ref[...])


def _mm_in_call(h, w, j, n_cols):
    m, k = h.shape
    tm, tn = 1024, 512
    return pl.pallas_call(
        _mm_in_kernel,
        out_shape=jax.ShapeDtypeStruct((m, n_cols), F32),
        grid=(n_cols // tn, m // tm),
        in_specs=[pl.BlockSpec((tm, k), lambda n, i: (i, 0)),
                  pl.BlockSpec((None, k, tn), lambda n, i: (j, 0, n))],
        out_specs=pl.BlockSpec((tm, tn), lambda n, i: (i, n)),
        scratch_shapes=[pltpu.VMEM((k, tn), BF16)],
        compiler_params=_cparams(2, 48),
        name="in_proj",
    )(h, w)


def _tail_kernel(h_ref, w_ref, wt_ref, o_ref, ot_ref):
    h = h_ref[...]
    o_ref[...] = _mm(h, w_ref[...].astype(BF16))
    ot_ref[...] = lax.dot_general(wt_ref[...].astype(BF16), h, (((1,), (1,)), ((), ())),
                                  preferred_element_type=F32)


def _tail_call(h, w, j, col0):
    m, k = h.shape
    tm = 1024
    wt = jnp.transpose(w[j, :, col0:col0 + 128])
    return pl.pallas_call(
        _tail_kernel,
        out_shape=(jax.ShapeDtypeStruct((m, 128), F32), jax.ShapeDtypeStruct((128, m), F32)),
        grid=(m // tm,),
        in_specs=[pl.BlockSpec((tm, k), lambda i: (i, 0)),
                  pl.BlockSpec((None, k, 128), lambda i: (j, 0, col0 // 128)),
                  pl.BlockSpec((128, k), lambda i: (0, 0))],
        out_specs=(pl.BlockSpec((tm, 128), lambda i: (i, 0)),
                   pl.BlockSpec((128, tm), lambda i: (0, i))),
        compiler_params=_cparams(1, 40),
        name="in_proj_tail",
    )(h, w, wt)


def _mm_out_kernel(ap_ref, as_ref, w_ref, x_ref, g_ref, o_ref, *, prompt_tiles):
    i = pl.program_id(1)

    def finish(acc):
        tm, tn = acc.shape
        o_ref[...] = x_ref[...] + g_ref[...] * acc.reshape(tm // ROWS, ROWS, tn)

    @pl.when(i < prompt_tiles)
    def _():
        finish(_mm(ap_ref[...], w_ref[...]))

    @pl.when(i >= prompt_tiles)
    def _():
        finish(_mm(as_ref[...], w_ref[...]))


def _mm_out_call(a_p, a_s, w_bf16, x3, gate_e):
    mp, k = a_p.shape
    ms = a_s.shape[0]
    d = w_bf16.shape[1]
    tm, tn = 512, 512
    pt, st = mp // tm, ms // tm
    return pl.pallas_call(
        functools.partial(_mm_out_kernel, prompt_tiles=pt),
        out_shape=jax.ShapeDtypeStruct(x3.shape, F32),
        grid=(d // tn, pt + st),
        in_specs=[pl.BlockSpec((tm, k), lambda n, i: (jnp.minimum(i, pt - 1), 0)),
                  pl.BlockSpec((tm, k), lambda n, i: (jnp.maximum(i - pt, 0), 0)),
                  pl.BlockSpec((k, tn), lambda n, i: (0, n), pipeline_mode=pl.Buffered(1)),
                  pl.BlockSpec((tm // ROWS, ROWS, tn), lambda n, i: (i, 0, n)),
                  pl.BlockSpec((tm // ROWS, 1, tn), lambda n, i: (i, 0, n))],
        out_specs=pl.BlockSpec((tm // ROWS, ROWS, tn), lambda n, i: (i, 0, n)),
        compiler_params=_cparams(2, 56),
        name="out_proj",
    )(a_p, a_s, w_bf16, x3, gate_e)


def _conv_prompt_kernel(*refs, tl, has_bias):
    if has_bias:
        x_ref, prev_ref, w_ref, b_ref, o_ref, xs_ref = refs
    else:
        x_ref, prev_ref, w_ref, o_ref, xs_ref = refs
    first = pl.program_id(1) == 0
    xs_ref[0:8, :] = jnp.where(first, 0.0, prev_ref[...])
    xs_ref[8:8 + tl, :] = x_ref[...]
    w = w_ref[...]
    y = w[3:4, :] * xs_ref[8:8 + tl, :]
    for kk in range(CONV_W - 1):
        y = y + w[kk:kk + 1, :] * xs_ref[5 + kk:5 + kk + tl, :]
    if has_bias:
        y = y + b_ref[...]
    o_ref[...] = _silu(y)


def _conv_prompt_call(proj, w_conv, bias, j, col0, width, batch, seq):
    tl, tc = 512, 2048
    lt = seq // tl
    has_bias = bias is not None
    in_specs = [pl.BlockSpec((tl, tc), lambda b, l, ct: (b * lt + l, ct + col0 // tc)),
                pl.BlockSpec((8, tc), lambda b, l, ct: (jnp.maximum((b * lt + l) * (tl // 8) - 1, 0),
                                                        ct + col0 // tc)),
                pl.BlockSpec((None, CONV_W, tc), lambda b, l, ct: (j, 0, ct))]
    args = [proj, proj, w_conv]
    if has_bias:
        in_specs.append(pl.BlockSpec((None, 1, tc), lambda b, l, ct: (j, 0, ct)))
        args.append(bias.reshape(bias.shape[0], 1, width))
    return pl.pallas_call(
        functools.partial(_conv_prompt_kernel, tl=tl, has_bias=has_bias),
        out_shape=jax.ShapeDtypeStruct((batch * seq, width), F32),
        grid=(batch, lt, width // tc),
        in_specs=in_specs,
        out_specs=pl.BlockSpec((tl, tc), lambda b, l, ct: (b * lt + l, ct)),
        scratch_shapes=[pltpu.VMEM((tl + 8, tc), F32)],
        compiler_params=_cparams(3, 40),
        name="conv_prompt",
    )(*args)


def _conv_sample_kernel(*refs, tb, seq, has_bias):
    if has_bias:
        x_ref, hist_ref, w_ref, b_ref, o_ref, xs_ref = refs
    else:
        x_ref, hist_ref, w_ref, o_ref, xs_ref = refs
    tc = x_ref.shape[1]
    xs_ref[:, 5:8, :] = hist_ref[...]
    xs_ref[:, 8:8 + seq, :] = x_ref[...].reshape(tb, seq, tc)
    w = w_ref[...]
    y = w[3:4, :] * xs_ref[:, 8:8 + seq, :]
    for kk in range(CONV_W - 1):
        y = y + w[kk:kk + 1, :] * xs_ref[:, 5 + kk:5 + kk + seq, :]
    if has_bias:
        y = y + b_ref[...]
    o_ref[...] = _silu(y).reshape(tb * seq, tc)


def _conv_sample_call(proj, hist, w_conv, bias, j, row0, col0, width, batch, seq):
    tb, tc = 32, 2048
    has_bias = bias is not None
    in_specs = [pl.BlockSpec((tb * seq, tc), lambda i, ct: (i + row0 // (tb * seq), ct + col0 // tc)),
                pl.BlockSpec((None, tb, CONV_W - 1, tc), lambda i, ct: (j, i, 0, ct)),
                pl.BlockSpec((None, CONV_W, tc), lambda i, ct: (j, 0, ct))]
    args = [proj, hist, w_conv]
    if has_bias:
        in_specs.append(pl.BlockSpec((None, 1, tc), lambda i, ct: (j, 0, ct)))
        args.append(bias.reshape(bias.shape[0], 1, width))
    return pl.pallas_call(
        functools.partial(_conv_sample_kernel, tb=tb, seq=seq, has_bias=has_bias),
        out_shape=jax.ShapeDtypeStruct((batch * seq, width), F32),
        grid=(batch // tb, width // tc),
        in_specs=in_specs,
        out_specs=pl.BlockSpec((tb * seq, tc), lambda i, ct: (i, ct)),
        scratch_shapes=[pltpu.VMEM((tb, 8 + seq, tc), F32)],
        compiler_params=_cparams(2, 40),
        name="conv_sample",
    )(*args)


def _fill_other_slots(stack_ref, slot):
    @pl.when(pl.program_id(2) == pl.num_programs(2) - 1)
    def _():
        for other in range(stack_ref.shape[0]):
            if other != slot:
                stack_ref[other] = stack_ref[slot]


def _dn_kernel(*refs, c, n_ch, hb, has_state, slot):
    q_ref, k_ref, v_ref, z_ref, gcol_ref, grow_ref, prow_ref, pcol_ref, nw_ref = refs[:9]
    s0_ref = refs[9] if has_state else None
    o_ref, sout_ref, grc_ref = refs[-3:]
    s_ref = sout_ref if slot is None else sout_ref.at[slot]
    hblk = pl.program_id(1)

    @pl.when(pl.program_id(2) == 0)
    def _():
        if has_state:
            s_ref[...] = s0_ref[...]
        else:
            s_ref[...] = jnp.zeros(s_ref.shape, F32)

    row = lax.broadcasted_iota(jnp.int32, (c, c), 0)
    col = lax.broadcasted_iota(jnp.int32, (c, c), 1)
    causal = row >= col
    strict = row > col
    tri = jnp.where(causal, 1.0, 0.0).astype(BF16)
    tri_t = jnp.where(row <= col, 1.0, 0.0).astype(BF16)
    lane = lax.broadcasted_iota(jnp.int32, (c, 128), 1)
    neg_a_row = -jnp.exp(prow_ref[0:1, :])
    dtb_row = prow_ref[1:2, :]
    pc = pcol_ref[...]
    neg_a_col = -jnp.exp(pc[:, 0:1])
    dtb_col = pc[:, 1:2]
    nw = nw_ref[...]
    nhalf = 64

    def chunk(ci, carry):
        r0 = pl.multiple_of(ci * c, c)
        gt = gcol_ref[pl.ds(r0, c), :]
        beta_all = jax.nn.sigmoid(gt)
        gcum_all = _cumsum_rows(tri, neg_a_row * _softplus(gt + dtb_row))
        grc_ref[...] = _cumsum_cols(neg_a_col * _softplus(grow_ref[ci] + dtb_col), tri_t)
        kns, kqs_in, kks, qks = [], [], [], []
        for hh in range(hb):
            q = q_ref[pl.ds(r0, c), hh * DN_DK:(hh + 1) * DN_DK]
            k = k_ref[pl.ds(r0, c), hh * DN_DK:(hh + 1) * DN_DK]
            qn = q * (lax.rsqrt(jnp.sum(q * q, axis=-1, keepdims=True) + EPS) * (DN_DK ** -0.5))
            kn = k * lax.rsqrt(jnp.sum(k * k, axis=-1, keepdims=True) + EPS)
            kq = jnp.concatenate([kn, qn], axis=0)
            kkqk = _dot_nt(kq, kn)
            kns.append(kn)
            kqs_in.append(kq)
            kks.append(kkqk[:c])
            qks.append(kkqk[c:])
        heads = [(hh, e) for hh in range(hb) for e in range(2)]
        betas, gcums, decays, lms, kqss = [], [], [], [], []
        for hh, e in heads:
            vh = 2 * (hblk * hb + hh) + e
            beta = jnp.sum(jnp.where(lane == vh, beta_all, 0.0), axis=1, keepdims=True)
            gcum = jnp.sum(jnp.where(lane == nhalf + vh, gcum_all, 0.0), axis=1, keepdims=True)
            diff = gcum - grc_ref[pl.ds(nhalf + vh, 1), :]
            decay = jnp.where(causal, jnp.exp(jnp.where(causal, diff, 0.0)), 0.0)
            eg = jnp.exp(gcum)
            betas.append(beta)
            gcums.append(gcum)
            decays.append(decay)
            lms.append(jnp.where(strict, beta * kks[hh] * decay, 0.0))
            kqss.append(_dot(kqs_in[hh] * jnp.concatenate([eg, eg], axis=0), s_ref[2 * hh + e]))
        ws = [betas[i] * (v_ref[pl.ds(r0, c), (2 * hh + e) * DN_DV:(2 * hh + e + 1) * DN_DV] - kqss[i][:c])
              for i, (hh, e) in enumerate(heads)]
        us = _unit_lower_solve_many(lms, ws, c, row, col)
        for i, (hh, e) in enumerate(heads):
            sl = slice((2 * hh + e) * DN_DV, (2 * hh + e + 1) * DN_DV)
            ub = us[i].astype(BF16)
            o = kqss[i][c:] + _mm((qks[hh] * decays[i]).astype(BF16), ub)
            gl = gcums[i][c - 1:c, :]
            s_ref[2 * hh + e] = (s_ref[2 * hh + e] * jnp.exp(gl)
                                 + _dot_tn(kns[hh] * jnp.exp(gl - gcums[i]), ub))
            on = o * lax.rsqrt(jnp.mean(o * o, axis=-1, keepdims=True) + EPS) * nw
            o_ref[pl.ds(r0, c), sl] = (on * _silu(z_ref[pl.ds(r0, c), sl])).astype(o_ref.dtype)
        return carry

    lax.fori_loop(0, n_ch, chunk, 0)
    if slot is not None:
        _fill_other_slots(sout_ref, slot)


def _dn_call(qkv, proj, tail, grow_t, prow, pcol, nw, s0, stack, j, *, batch, seq, c, tl, hb, row0,
             n_hk, n_layers, out_dtype):
    lt = seq // tl
    key = n_hk * DN_DK
    z0 = 2 * key + 2 * key
    rb0 = row0 // tl
    has_state = s0 is not None

    def rows(b, l):
        return b * lt + l

    in_specs = [
        pl.BlockSpec((tl, hb * DN_DK), lambda b, h, l: (rows(b, l), h)),
        pl.BlockSpec((tl, hb * DN_DK), lambda b, h, l: (rows(b, l), h + key // (hb * DN_DK))),
        pl.BlockSpec((tl, 2 * hb * DN_DV), lambda b, h, l: (rows(b, l), h + 2 * key // (2 * hb * DN_DV))),
        pl.BlockSpec((tl, 2 * hb * DN_DV), lambda b, h, l: (rows(b, l) + rb0, h + z0 // (2 * hb * DN_DV))),
        pl.BlockSpec((tl, 128), lambda b, h, l: (rows(b, l) + rb0, 0)),
        pl.BlockSpec((None, tl // c, 128, c), lambda b, h, l: (b, l, 0, 0)),
        pl.BlockSpec((2, 128), lambda b, h, l: (0, 0)),
        pl.BlockSpec((128, 2), lambda b, h, l: (0, 0)),
        pl.BlockSpec((1, DN_DV), lambda b, h, l: (0, 0)),
    ]
    args = [qkv, qkv, qkv, proj, tail, grow_t, prow, pcol, nw]
    if has_state:
        in_specs.append(pl.BlockSpec((None, None, 2 * hb, DN_DK, DN_DV), lambda b, h, l: (j, b, h, 0, 0)))
        args.append(s0)
    if stack is None:
        aliases, slot = {}, j
        s_spec = pl.BlockSpec((n_layers, None, 2 * hb, DN_DK, DN_DV), lambda b, h, l: (0, b, h, 0, 0))
    else:
        aliases, slot = {len(args): 1}, None
        s_spec = pl.BlockSpec((None, None, 2 * hb, DN_DK, DN_DV), lambda b, h, l: (j, b, h, 0, 0))
        in_specs.append(pl.BlockSpec(memory_space=pl.ANY))
        args.append(stack)
    return pl.pallas_call(
        functools.partial(_dn_kernel, c=c, n_ch=tl // c, hb=hb, has_state=has_state, slot=slot),
        out_shape=(jax.ShapeDtypeStruct((batch * seq, 2 * key), out_dtype),
                   jax.ShapeDtypeStruct((n_layers, batch, 2 * n_hk, DN_DK, DN_DV), F32)),
        grid=(batch, n_hk // hb, lt),
        in_specs=in_specs,
        out_specs=(pl.BlockSpec((tl, 2 * hb * DN_DV), lambda b, h, l: (rows(b, l), h)), s_spec),
        scratch_shapes=[pltpu.VMEM((128, c), F32)],
        input_output_aliases=aliases,
        compiler_params=_cparams(3, 56),
        name="gated_delta",
    )(*args)


def _ssd_kernel(*refs, c, n_ch, gb, has_state, slot):
    x_ref, b_ref, c_ref, z_ref, dcol_ref, drow_ref, prow_ref, pcol_ref, dx_ref, nw_ref = refs[:10]
    h0_ref = refs[10] if has_state else None
    o_ref, hout_ref = refs[-2:]
    h_ref = hout_ref if slot is None else hout_ref.at[slot]
    r_heads = h_ref.shape[0] // gb
    n_pairs = r_heads // 2
    inner = r_heads * SSM_P
    pw = 2 * SSM_P
    c2 = 2 * c

    @pl.when(pl.program_id(2) == 0)
    def _():
        if has_state:
            h_ref[...] = h0_ref[...]
        else:
            h_ref[...] = jnp.zeros(h_ref.shape, F32)

    row = lax.broadcasted_iota(jnp.int32, (c, c), 0)
    col = lax.broadcasted_iota(jnp.int32, (c, c), 1)
    causal = row >= col
    tri = jnp.where(causal, 1.0, 0.0).astype(BF16)
    row2 = lax.broadcasted_iota(jnp.int32, (c2, c2), 0)
    col2 = lax.broadcasted_iota(jnp.int32, (c2, c2), 1)
    tri2_t = jnp.where((row2 <= col2) & ((row2 // c) == (col2 // c)), 1.0, 0.0).astype(BF16)
    prow_i = lax.broadcasted_iota(jnp.int32, (c, c2), 0)
    pcol_i = lax.broadcasted_iota(jnp.int32, (c, c2), 1)
    causal2 = prow_i >= (pcol_i % c)
    eh = lax.broadcasted_iota(jnp.int32, (r_heads, inner), 0)
    el = lax.broadcasted_iota(jnp.int32, (r_heads, inner), 1)
    spread_x = jnp.where(el // SSM_P == eh, 1.0, 0.0).astype(BF16)
    if c == SSM_P:
        spread_m = spread_x
    else:
        eh2 = lax.broadcasted_iota(jnp.int32, (r_heads, r_heads * c), 0)
        el2 = lax.broadcasted_iota(jnp.int32, (r_heads, r_heads * c), 1)
        spread_m = jnp.where(el2 // c == eh2, 1.0, 0.0).astype(BF16)
    lane_p = lax.broadcasted_iota(jnp.int32, (c, pw), 1)
    left = lane_p < SSM_P

    def chunk(ci, carry):
        r0 = pl.multiple_of(ci * c, c)
        for gg in range(gb):
            pr = prow_ref[gg]
            dtc = _softplus(dcol_ref[gg, pl.ds(r0, c), :] + pr[1:2, :])
            cs = _cumsum_rows(tri, dtc * (-jnp.exp(pr[0:1, :])))
            cl = cs[c - 1:c, :]
            wgt = jnp.exp(cl - cs) * dtc
            hi, mid, lo = _split3(cs)
            cs3 = jnp.concatenate([hi.astype(F32), mid.astype(F32), lo.astype(F32)], axis=0)
            oth = jnp.concatenate([dtc, wgt, jnp.exp(cs)], axis=0)
            if c == SSM_P:
                ex = _mm(jnp.concatenate([cs3, oth], axis=0).astype(BF16), spread_x)
                cs_m = ex[0:c] + (ex[c:2 * c] + ex[2 * c:3 * c])
                ex = ex[3 * c:]
            else:
                exm = _mm(cs3.astype(BF16), spread_m)
                cs_m = exm[0:c] + (exm[c:2 * c] + exm[2 * c:3 * c])
                ex = _mm(oth.astype(BF16), spread_x)
            dt_x, w_x, ecs_x = ex[0:c], ex[c:2 * c], ex[2 * c:3 * c]
            pcr = pcol_ref[gg]
            dtr = _softplus(drow_ref[gg, ci] + pcr[1])
            csr = _cumsum_cols(dtr * (-jnp.exp(pcr[0])), tri2_t)
            bc = b_ref[pl.ds(r0, c), gg * SSM_N:(gg + 1) * SSM_N]
            cc = c_ref[pl.ds(r0, c), gg * SSM_N:(gg + 1) * SSM_N]
            cbm = jnp.where(causal, _dot_nt(cc, bc), 0.0)
            cb2 = jnp.concatenate([cbm, cbm], axis=1)
            h_all = h_ref[gg * r_heads:(gg + 1) * r_heads].reshape(inner, SSM_N)
            yoff = _dot_nt(cc, h_all)
            gsl = slice(gg * inner, (gg + 1) * inner)
            x_all = x_ref[pl.ds(r0, c), gsl]
            xdt = x_all * dt_x
            ys = []
            for p in range(n_pairs):
                diff = cs_m[:, p * c2:(p + 1) * c2] - csr[p:p + 1, :]
                seg = jnp.where(causal2, jnp.exp(jnp.where(causal2, diff, 0.0)), 0.0)
                xp = xdt[:, p * pw:(p + 1) * pw]
                rhs = jnp.concatenate([jnp.where(left, xp, 0.0), jnp.where(left, 0.0, xp)], axis=0)
                ys.append(_dot(cb2 * seg, rhs))
            y = jnp.concatenate(ys, axis=1) + yoff * ecs_x + dx_ref[gg] * x_all
            upd = _dot_tn(x_all * w_x, bc)
            for r in range(r_heads):
                hd = gg * r_heads + r
                h_ref[hd] = h_ref[hd] * jnp.exp(cl[:, r:r + 1]) + upd[r * SSM_P:(r + 1) * SSM_P]
            yz = y * _silu(z_ref[pl.ds(r0, c), gsl])
            on = yz * lax.rsqrt(jnp.mean(yz * yz, axis=-1, keepdims=True) + EPS) * nw_ref[:, gsl]
            o_ref[pl.ds(r0, c), gsl] = on.astype(o_ref.dtype)
        return carry

    lax.fori_loop(0, n_ch, chunk, 0)
    if slot is not None:
        _fill_other_slots(hout_ref, slot)


def _ssd_call(xbc, proj, dcol, drow, prow, pcol, dexp, nw, h0, stack, j, *, batch, seq, c, tl, gb, row0,
              n_heads, n_layers, out_dtype):
    lt = seq // tl
    r_heads = n_heads // SSM_GROUPS
    ginner = r_heads * SSM_P
    inner = n_heads * SSM_P
    rb0 = row0 // tl
    has_state = h0 is not None

    def rows(b, l):
        return b * lt + l

    in_specs = [
        pl.BlockSpec((tl, gb * ginner), lambda b, g, l: (rows(b, l), g)),
        pl.BlockSpec((tl, gb * SSM_N), lambda b, g, l: (rows(b, l), g + inner // (gb * SSM_N))),
        pl.BlockSpec((tl, gb * SSM_N),
                     lambda b, g, l: (rows(b, l), g + (inner + SSM_GROUPS * SSM_N) // (gb * SSM_N))),
        pl.BlockSpec((tl, gb * ginner), lambda b, g, l: (rows(b, l) + rb0, g)),
        pl.BlockSpec((None, gb, tl, r_heads), lambda b, g, l: (b, g, l, 0)),
        pl.BlockSpec((None, gb, tl // c, r_heads // 2, 2 * c), lambda b, g, l: (b, g, l, 0, 0)),
        pl.BlockSpec((gb, 2, r_heads), lambda b, g, l: (g, 0, 0)),
        pl.BlockSpec((gb, 2, r_heads // 2, 2 * c), lambda b, g, l: (g, 0, 0, 0)),
        pl.BlockSpec((gb, 1, ginner), lambda b, g, l: (g, 0, 0)),
        pl.BlockSpec((1, gb * ginner), lambda b, g, l: (0, g)),
    ]
    args = [xbc, xbc, xbc, proj, dcol, drow, prow, pcol, dexp, nw]
    if has_state:
        in_specs.append(pl.BlockSpec((None, None, gb * r_heads, SSM_P, SSM_N),
                                     lambda b, g, l: (j, b, g, 0, 0)))
        args.append(h0)
    if stack is None:
        aliases, slot = {}, j
        h_spec = pl.BlockSpec((n_layers, None, gb * r_heads, SSM_P, SSM_N), lambda b, g, l: (0, b, g, 0, 0))
    else:
        aliases, slot = {len(args): 1}, None
        h_spec = pl.BlockSpec((None, None, gb * r_heads, SSM_P, SSM_N), lambda b, g, l: (j, b, g, 0, 0))
        in_specs.append(pl.BlockSpec(memory_space=pl.ANY))
        args.append(stack)
    return pl.pallas_call(
        functools.partial(_ssd_kernel, c=c, n_ch=tl // c, gb=gb, has_state=has_state, slot=slot),
        out_shape=(jax.ShapeDtypeStruct((batch * seq, inner), out_dtype),
                   jax.ShapeDtypeStruct((n_layers, batch, n_heads, SSM_P, SSM_N), F32)),
        grid=(batch, SSM_GROUPS // gb, lt),
        in_specs=in_specs,
        out_specs=(pl.BlockSpec((tl, gb * ginner), lambda b, g, l: (rows(b, l), g)), h_spec),
        input_output_aliases=aliases,
        compiler_params=_cparams(3, 48),
        name="ssd",
    )(*args)


def kernel(x_prompt, x_sample, state_dn_S, state_dn_conv, state_ssm, state_ssm_conv, c_prompt, c_sample,
           w_ada, b_ada, norm_w, dn_w_in, dn_w_conv, dn_a_log, dn_dt_bias, dn_norm_w, dn_w_out,
           ssm_w_in, ssm_w_conv, ssm_conv_b, ssm_a_log, ssm_dt_bias, ssm_d, ssm_norm_w, ssm_w_out,
           final_norm_w):
    bp, lp, d = x_prompt.shape
    bs, ls, _ = x_sample.shape
    mp, ms = bp * lp, bs * ls
    m = mp + ms
    depth = w_ada.shape[0]
    n_hv = dn_a_log.shape[1]
    n_hk = n_hv // 2
    dn_key = n_hk * DN_DK
    dn_conv = 2 * dn_key + n_hv * DN_DV
    dn_main = dn_conv + n_hv * DN_DV
    n_sh = ssm_a_log.shape[1]
    r_heads = n_sh // SSM_GROUPS
    ssm_inner = n_sh * SSM_P
    ssm_conv = ssm_inner + 2 * SSM_GROUPS * SSM_N
    ssm_main = ssm_inner + ssm_conv
    cp = min(CHUNK, lp)
    cs_ = min(CHUNK, ls)
    assert lp % cp == 0 and ls == cs_ and ls % ROWS == 0 and lp % ROWS == 0
    assert ls >= CONV_W - 1 and lp >= CONV_W - 1 and 2 * n_hv == 128 and n_sh == 128

    x3 = jnp.concatenate([x_prompt.reshape(mp, d), x_sample.reshape(ms, d)], axis=0).reshape(m // ROWS, ROWS, d)

    nb = bp + bs
    pad = (-nb) % 16
    c_all = jnp.concatenate([c_prompt, c_sample, jnp.zeros((pad, d), F32)], axis=0)
    mod = _mod_call(c_all, w_ada, b_ada)

    def expand(t):
        return jnp.concatenate([jnp.repeat(t[:bp], lp // ROWS, axis=0),
                                jnp.repeat(t[bp:nb], ls // ROWS, axis=0)], axis=0).reshape(m // ROWS, 1, d)

    n_dn = dn_w_in.shape[0]
    n_ssm = ssm_w_in.shape[0]
    kw = CONV_W - 1
    dn_cv_p, ssm_cv_p, dn_cv_s, ssm_cv_s = [], [], [], []
    dn_s_p = dn_s_s = ssm_h_p = ssm_h_s = None

    def last_rows(proj, col0, col1):
        p = jnp.stack([lax.slice(proj, (b * lp + lp - kw, col0), ((b + 1) * lp, col1)) for b in range(bp)])
        s = lax.slice(proj, (mp, col0), (m, col1)).reshape(bs, ls, col1 - col0)[:, ls - kw:]
        return p, s

    for i in range(depth):
        j = i // 2
        shift_e = expand(mod[i, :, :d])
        scale_e = expand(mod[i, :, d:2 * d])
        gate_e = expand(mod[i, :, 2 * d:])
        h = _prenorm_call(x3, norm_w[i], scale_e, shift_e)
        if i % 2 == 0:
            proj = _mm_in_call(h, dn_w_in, j, dn_main)
            tail, tail_t = _tail_call(h, dn_w_in, j, dn_main)
            qkv_p = _conv_prompt_call(proj, dn_w_conv, None, j, 0, dn_conv, bp, lp)
            qkv_s = _conv_sample_call(proj, state_dn_conv, dn_w_conv, None, j, mp, 0, dn_conv, bs, ls)
            grow_p = tail_t[:, :mp].reshape(128, bp, lp // cp, cp).transpose(1, 2, 0, 3)
            grow_s = tail_t[:, mp:].reshape(128, bs, 1, cs_).transpose(1, 2, 0, 3)
            prow = jnp.concatenate([jnp.zeros((2, n_hv), F32), jnp.stack([dn_a_log[j], dn_dt_bias[j]])], axis=1)
            pcol = prow.T
            nw = dn_norm_w[j].reshape(1, DN_DV)
            o_p, dn_s_p = _dn_call(qkv_p, proj, tail, grow_p, prow, pcol, nw, None, dn_s_p, j, batch=bp,
                                   seq=lp, c=cp, tl=512, hb=8, row0=0, n_hk=n_hk, n_layers=n_dn,
                                   out_dtype=BF16)
            o_s, dn_s_s = _dn_call(qkv_s, proj, tail, grow_s, prow, pcol, nw, state_dn_S, dn_s_s, j, batch=bs,
                                   seq=ls, c=cs_, tl=ls, hb=16, row0=mp, n_hk=n_hk, n_layers=n_dn,
                                   out_dtype=F32)
            cv_p, cv_s = last_rows(proj, 0, dn_conv)
            dn_cv_p.append(cv_p)
            dn_cv_s.append(cv_s)
            w_out = dn_w_out[j].astype(BF16)
        else:
            proj = _mm_in_call(h, ssm_w_in, j, ssm_main)
            tail, tail_t = _tail_call(h, ssm_w_in, j, ssm_main)
            xbc_p = _conv_prompt_call(proj, ssm_w_conv, ssm_conv_b, j, ssm_inner, ssm_conv, bp, lp)
            xbc_s = _conv_sample_call(proj, state_ssm_conv, ssm_w_conv, ssm_conv_b, j, mp, ssm_inner,
                                      ssm_conv, bs, ls)
            npair = r_heads // 2
            dcol_p = tail[:mp].reshape(bp, lp, SSM_GROUPS, r_heads).transpose(0, 2, 1, 3)
            dcol_s = tail[mp:].reshape(bs, ls, SSM_GROUPS, r_heads).transpose(0, 2, 1, 3)
            drow_p = tail_t[:, :mp].reshape(SSM_GROUPS, npair, 2, bp, lp // cp, cp).transpose(3, 0, 4, 1, 2, 5)
            drow_p = drow_p.reshape(bp, SSM_GROUPS, lp // cp, npair, 2 * cp)
            drow_s = tail_t[:, mp:].reshape(SSM_GROUPS, npair, 2, bs, 1, cs_).transpose(3, 0, 4, 1, 2, 5)
            drow_s = drow_s.reshape(bs, SSM_GROUPS, 1, npair, 2 * cs_)
            par = jnp.stack([ssm_a_log[j], ssm_dt_bias[j]])
            prow = par.reshape(2, SSM_GROUPS, r_heads).transpose(1, 0, 2)
            par_pairs = par.reshape(2, SSM_GROUPS, npair, 2, 1).transpose(1, 0, 2, 3, 4)
            pcol_p = jnp.broadcast_to(par_pairs, (SSM_GROUPS, 2, npair, 2, cp)).reshape(SSM_GROUPS, 2, npair, 2 * cp)
            pcol_s = jnp.broadcast_to(par_pairs, (SSM_GROUPS, 2, npair, 2, cs_)).reshape(SSM_GROUPS, 2, npair, 2 * cs_)
            dexp = jnp.repeat(ssm_d[j], SSM_P).reshape(SSM_GROUPS, 1, r_heads * SSM_P)
            nw = ssm_norm_w[j].reshape(1, ssm_inner)
            o_p, ssm_h_p = _ssd_call(xbc_p, proj, dcol_p, drow_p, prow, pcol_p, dexp, nw, None, ssm_h_p, j,
                                     batch=bp, seq=lp, c=cp, tl=512, gb=1, row0=0, n_heads=n_sh,
                                     n_layers=n_ssm, out_dtype=BF16)
            o_s, ssm_h_s = _ssd_call(xbc_s, proj, dcol_s, drow_s, prow, pcol_s, dexp, nw, state_ssm, ssm_h_s, j,
                                     batch=bs, seq=ls, c=cs_, tl=ls, gb=4, row0=mp, n_heads=n_sh,
                                     n_layers=n_ssm, out_dtype=F32)
            cv_p, cv_s = last_rows(proj, ssm_inner, ssm_main)
            ssm_cv_p.append(cv_p)
            ssm_cv_s.append(cv_s)
            w_out = ssm_w_out[j].astype(BF16)
        x3 = _mm_out_call(o_p, o_s.astype(BF16), w_out, x3, gate_e)

    x2 = x3.reshape(m, d)
    y_p = _final_norm_call(x2, final_norm_w, 0, mp).reshape(bp, lp, d)
    y_s = _final_norm_call(x2, final_norm_w, mp, ms).reshape(bs, ls, d)
    return (y_p, y_s, dn_s_p, jnp.stack(dn_cv_p), ssm_h_p, jnp.stack(ssm_cv_p),
            dn_s_s, jnp.stack(dn_cv_s), ssm_h_s, jnp.stack(ssm_cv_s))
```
